```python
import jax, jax.numpy as jnp
from jax import lax
import numpy as np

D_MODEL = 1024
BATCH = 4
SEQ = 4096
DEPTH = 4
DEC_BATCH = 16
DEC_SEQ = 16
PAST_LEN = 2048

CHUNK = 64
HEAD_DIM = 64
D_MIX = D_MODEL
H_M = 4
GROUP_HEADS = (D_MIX // HEAD_DIM - H_M) // 3
H_A = GROUP_HEADS
H_B = GROUP_HEADS
H_C = GROUP_HEADS
W_A = H_A * HEAD_DIM
W_B = H_B * HEAD_DIM
W_C = H_C * HEAD_DIM
W_M = H_M * HEAD_DIM
ROT_DIM = HEAD_DIM // 4
ROPE_THETA = 500000.0
H_IDX = 4
D_IDX = HEAD_DIM
TOPK_MAX = 256
Q_BLOCK = 128
BAND_CHUNKS = 8
BAND = BAND_CHUNKS * CHUNK
REL_CLIP = 128
N_MEM = 256
EPS = 1e-6
F32 = jnp.float32

IN_SPLITS = (('a_q', W_A), ('a_k', HEAD_DIM), ('a_v', HEAD_DIM), ('a_qi', H_IDX * D_IDX), ('a_ki', D_IDX),
             ('a_wi', H_IDX), ('a_g', W_A),
             ('b_q', W_B), ('b_k', W_B), ('b_v', W_B), ('b_f', H_B), ('b_g', W_B),
             ('c_q', W_C), ('c_k', W_C), ('c_v', W_C), ('c_g', W_C),
             ('m_q', W_M), ('m_g', W_M))
D_IN = sum(size for _, size in IN_SPLITS)

kernel_name = 'hybrid_streaming_encoder_step'


def rmsnorm(x, g):
    xf = x.astype(F32)
    y = xf * lax.rsqrt(jnp.mean(xf * xf, axis=-1, keepdims=True) + EPS)
    return (y * g.astype(F32)).astype(x.dtype)


def rope_partial(x, pos):
    half = ROT_DIM // 2
    inv = ROPE_THETA ** (-jnp.arange(half, dtype=F32) / half)
    ang = pos.astype(F32)[:, None] * inv[None, :]
    shape = (ang.shape[0],) + (1,) * (x.ndim - 3) + (half,)
    cos = jnp.cos(ang).reshape(shape)
    sin = jnp.sin(ang).reshape(shape)
    xr = x[..., :ROT_DIM].astype(F32)
    x1, x2 = xr[..., :half], xr[..., half:]
    rot = jnp.concatenate([x1 * cos - x2 * sin, x2 * cos + x1 * sin], axis=-1).astype(x.dtype)
    return jnp.concatenate([rot, x[..., ROT_DIM:]], axis=-1)


def split_proj(p):
    offs = np.cumsum([s for _, s in IN_SPLITS])[:-1].tolist()
    return dict(zip([n for n, _ in IN_SPLITS], jnp.split(p, offs, axis=-1)))


def a_inputs(p, pos):
    lead = p['a_q'].shape[:2]
    qa = rope_partial(p['a_q'].reshape(lead + (H_A, HEAD_DIM)), pos)
    ka = rope_partial(p['a_k'], pos)
    qi = rope_partial(p['a_qi'].reshape(lead + (H_IDX, D_IDX)), pos)
    ki = rope_partial(p['a_ki'], pos)
    return qa, ka, p['a_v'], qi, ki, p['a_wi']


def b_inputs(p, bf):
    lead = p['b_q'].shape[:2]
    heads = lambda t: t.reshape(lead + (H_B, HEAD_DIM))
    lf = jax.nn.log_sigmoid(p['b_f'].astype(F32) + bf.astype(F32))
    return heads(p['b_q']), heads(p['b_k']), heads(p['b_v']), lf


def c_inputs(p):
    lead = p['c_q'].shape[:2]
    heads = lambda t: t.reshape(lead + (H_C, HEAD_DIM))
    return heads(p['c_q']), heads(p['c_k']), heads(p['c_v'])


def m_query(p):
    return p['m_q'].reshape(p['m_q'].shape[:2] + (H_M, HEAD_DIM))


def mem_kv(mem, g, w):
    kv = rmsnorm(mem, g) @ w
    lead = kv.shape[:2]
    mk, mv = jnp.split(kv, 2, axis=-1)
    return mk.reshape(lead + (H_M, HEAD_DIM)), mv.reshape(lead + (H_M, HEAD_DIM))


def dsa_attend(q, qi, wi, lim, k, v, ki, topk):
    L = k.shape[1]
    dots = jnp.einsum('bqhd,bld->bqhl', qi.astype(F32), ki.astype(F32)) * D_IDX ** -0.5
    score = jnp.einsum('bqh,bqhl->bql', wi.astype(F32) * H_IDX ** -0.5, jax.nn.relu(dots))
    admissible = jnp.arange(L)[None, :] < lim[:, None]
    score = jnp.where(admissible[None], score, -jnp.inf)
    top_s, top_i = lax.top_k(score, topk)
    valid = jnp.isfinite(top_s)
    ks = jax.vmap(lambda kk, ii: kk[ii])(k, top_i)
    vs = jax.vmap(lambda vv, ii: vv[ii])(v, top_i)
    logits = jnp.einsum('bqhd,bqkd->bqhk', q.astype(F32), ks.astype(F32)) * HEAD_DIM ** -0.5
    logits = jnp.where(valid[:, :, None, :], logits, -jnp.inf)
    prob = jax.nn.softmax(logits, axis=-1)
    return jnp.einsum('bqhk,bqkd->bqhd', prob, vs.astype(F32)).astype(q.dtype)


def dsa_prompt(q, qi, wi, k, v, ki):
    B, S = q.shape[:2]
    nb = S // Q_BLOCK
    pos = jnp.arange(S)
    lim = jnp.minimum((pos // CHUNK + 1) * CHUNK, S)
    topk = min(TOPK_MAX, S // 4)
    blk = lambda t: t.reshape((B, nb, Q_BLOCK) + t.shape[2:]).swapaxes(0, 1)
    o = lax.map(lambda a: dsa_attend(a[0], a[1], a[2], a[3], k, v, ki, topk),
                (blk(q), blk(qi), blk(wi), lim.reshape(nb, Q_BLOCK)))
    return o.swapaxes(0, 1).reshape(q.shape)


def fox_attend(q, k, v, fq, fk, qpos, kpos):
    logits = jnp.einsum('bqhd,bkhd->bhqk', q.astype(F32), k.astype(F32)) * HEAD_DIM ** -0.5
    logits = logits + jnp.swapaxes(fq, 1, 2)[..., :, None] - jnp.swapaxes(fk, 1, 2)[..., None, :]
    logits = jnp.where((kpos[None, :] <= qpos[:, None])[None, None], logits, -jnp.inf)
    prob = jax.nn.softmax(logits, axis=-1)
    return jnp.einsum('bhqk,bkhd->bqhd', prob, v.astype(F32)).astype(q.dtype)


def fox_prompt(q, k, v, F):
    B, S, H, D = q.shape
    nb = S // Q_BLOCK
    pos = jnp.arange(S)
    blk = lambda t: t.reshape((B, nb, Q_BLOCK) + t.shape[2:]).swapaxes(0, 1)
    o = lax.map(lambda a: fox_attend(a[0], k, v, a[1], F, a[2], pos),
                (blk(q), blk(F), pos.reshape(nb, Q_BLOCK)))
    return o.swapaxes(0, 1).reshape(q.shape)


def band_bias_mask(tab, qpos, kpos):
    rel = qpos[..., :, None] - kpos[..., None, :]
    idx = jnp.clip(rel, -REL_CLIP, REL_CLIP) + REL_CLIP
    bias = jnp.moveaxis(tab.astype(F32)[:, idx], 0, -3)
    qc = qpos[..., :, None] // CHUNK
    kc = kpos[..., None, :] // CHUNK
    mask = (kpos[..., None, :] >= 0) & (kc <= qc) & (kc >= qc - BAND_CHUNKS)
    return bias, mask


def band_attend(q, k, v, bias, mask):
    logits = jnp.einsum('...qhd,...khd->...hqk', q.astype(F32), k.astype(F32)) * HEAD_DIM ** -0.5 + bias
    logits = jnp.where(mask[..., None, :, :], logits, -jnp.inf)
    prob = jax.nn.softmax(logits, axis=-1)
    return jnp.einsum('...hqk,...khd->...qhd', prob, v.astype(F32)).astype(q.dtype)


def band_prompt(q, k, v, tab):
    B, S, H, D = q.shape
    nc = S // CHUNK
    def bands(t):
        tp = jnp.pad(t, ((0, 0), (BAND, 0), (0, 0), (0, 0))).reshape(B, nc + BAND_CHUNKS, CHUNK, H, D)
        return jnp.concatenate([tp[:, j:j + nc] for j in range(BAND_CHUNKS + 1)], axis=2)
    cidx = jnp.arange(nc)
    qpos = cidx[:, None] * CHUNK + jnp.arange(CHUNK)[None, :]
    kpos = (cidx[:, None] - BAND_CHUNKS) * CHUNK + jnp.arange((BAND_CHUNKS + 1) * CHUNK)[None, :]
    bias, mask = band_bias_mask(tab, qpos, kpos)
    o = band_attend(q.reshape(B, nc, CHUNK, H, D), bands(k), bands(v), bias, mask)
    return o.reshape(B, S, H, D)


def mem_attend(q, mk, mv):
    logits = jnp.einsum('bshd,bnhd->bhsn', q.astype(F32), mk.astype(F32)) * HEAD_DIM ** -0.5
    prob = jax.nn.softmax(logits, axis=-1)
    return jnp.einsum('bhsn,bnhd->bshd', prob, mv.astype(F32)).astype(q.dtype)


def merge(p, o_a, o_b, o_c, o_m, w_o):
    lead = o_a.shape[:2]
    parts = [o.reshape(lead + (-1,)) * jax.nn.silu(p[g])
             for o, g in ((o_a, 'a_g'), (o_b, 'b_g'), (o_c, 'c_g'), (o_m, 'm_g'))]
    return jnp.concatenate(parts, axis=-1) @ w_o


def setup_inputs(seed: int = 0) -> dict:
    key = jax.random.key(seed)
    ks = jax.random.split(key, 24)
    nrm = lambda k, shape, s=1.0: s * jax.random.normal(k, shape, F32)
    c_keep = min(BAND, PAST_LEN)
    return {
        'x_prompt': nrm(ks[0], (BATCH, SEQ, D_MODEL)),
        'x_sample': nrm(ks[1], (DEC_BATCH, DEC_SEQ, D_MODEL)),
        'cache_a_k': nrm(ks[2], (DEPTH, DEC_BATCH, PAST_LEN, HEAD_DIM)),
        'cache_a_v': nrm(ks[3], (DEPTH, DEC_BATCH, PAST_LEN, HEAD_DIM)),
        'cache_a_kidx': nrm(ks[4], (DEPTH, DEC_BATCH, PAST_LEN, D_IDX)),
        'cache_b_k': nrm(ks[5], (DEPTH, DEC_BATCH, PAST_LEN, H_B, HEAD_DIM)),
        'cache_b_v': nrm(ks[6], (DEPTH, DEC_BATCH, PAST_LEN, H_B, HEAD_DIM)),
        'cache_b_logf': jax.nn.log_sigmoid(nrm(ks[7], (DEPTH, DEC_BATCH, PAST_LEN, H_B))),
        'cache_c_k': nrm(ks[8], (DEPTH, DEC_BATCH, c_keep, H_C, HEAD_DIM)),
        'cache_c_v': nrm(ks[9], (DEPTH, DEC_BATCH, c_keep, H_C, HEAD_DIM)),
        'cache_mem_k': nrm(ks[10], (DEPTH, DEC_BATCH, N_MEM, H_M, HEAD_DIM)),
        'cache_mem_v': nrm(ks[11], (DEPTH, DEC_BATCH, N_MEM, H_M, HEAD_DIM)),
        'mem_prompt': nrm(ks[12], (BATCH, N_MEM, D_MODEL)),
        'w_in': nrm(ks[13], (DEPTH, D_MODEL, D_IN), D_MODEL ** -0.5),
        'w_out': nrm(ks[14], (DEPTH, D_MIX, D_MODEL), D_MIX ** -0.5),
        'g_norm': 1.0 + nrm(ks[15], (DEPTH, D_MODEL), 0.01),
        'b_f': nrm(ks[16], (DEPTH, H_B), 0.1),
        'c_bias': nrm(ks[17], (DEPTH, H_C, 2 * REL_CLIP + 1), 0.1),
        'g_mem': 1.0 + nrm(ks[18], (DEPTH, D_MODEL), 0.01),
        'w_mem_kv': nrm(ks[19], (DEPTH, D_MODEL, 2 * W_M), D_MODEL ** -0.5),
        'g_final': 1.0 + nrm(ks[20], (D_MODEL,), 0.01),
    }


def reference(x_prompt, x_sample, cache_a_k, cache_a_v, cache_a_kidx, cache_b_k, cache_b_v, cache_b_logf,
              cache_c_k, cache_c_v, cache_mem_k, cache_mem_v, mem_prompt,
              w_in, w_out, g_norm, b_f, c_bias, g_mem, w_mem_kv, g_final):
    x = x_prompt
    S = x.shape[1]
    pos = jnp.arange(S)
    c_keep_p = min(BAND, S)
    pa_k, pa_v, pa_ki, pb_k, pb_v, pb_lf, pc_k, pc_v, pm_k, pm_v = [], [], [], [], [], [], [], [], [], []
    for l in range(DEPTH):
        p = split_proj(rmsnorm(x, g_norm[l]) @ w_in[l])
        qa, ka, va, qi, ki, wi = a_inputs(p, pos)
        o_a = dsa_prompt(qa, qi, wi, ka, va, ki)
        qb, kb, vb, lf = b_inputs(p, b_f[l])
        o_b = fox_prompt(qb, kb, vb, jnp.cumsum(lf, axis=1))
        qc, kc, vc = c_inputs(p)
        o_c = band_prompt(qc, kc, vc, c_bias[l])
        mk, mv = mem_kv(mem_prompt, g_mem[l], w_mem_kv[l])
        o_m = mem_attend(m_query(p), mk, mv)
        x = x + merge(p, o_a, o_b, o_c, o_m, w_out[l])
        pa_k.append(ka); pa_v.append(va); pa_ki.append(ki)
        pb_k.append(kb); pb_v.append(vb); pb_lf.append(lf)
        pc_k.append(kc[:, S - c_keep_p:]); pc_v.append(vc[:, S - c_keep_p:])
        pm_k.append(mk); pm_v.append(mv)
    y_prompt = rmsnorm(x, g_final)

    x = x_sample
    T = x.shape[1]
    P = cache_a_k.shape[2]
    L = P + T
    qpos = P + jnp.arange(T)
    kpos_all = jnp.arange(L)
    c_keep = cache_c_k.shape[2]
    kpos_c = jnp.concatenate([P - c_keep + jnp.arange(c_keep), qpos])
    lim = jnp.minimum((qpos // CHUNK + 1) * CHUNK, L)
    topk = min(TOPK_MAX, L // 4)
    cat = lambda a, b: jnp.concatenate([a, b.astype(a.dtype)], axis=1)
    sa_k, sa_v, sa_ki, sb_k, sb_v, sb_lf, sc_k, sc_v = [], [], [], [], [], [], [], []
    for l in range(DEPTH):
        p = split_proj(rmsnorm(x, g_norm[l]) @ w_in[l])
        qa, ka, va, qi, ki, wi = a_inputs(p, qpos)
        o_a = dsa_attend(qa, qi, wi, lim, cat(cache_a_k[l], ka), cat(cache_a_v[l], va),
                         cat(cache_a_kidx[l], ki), topk)
        qb, kb, vb, lf = b_inputs(p, b_f[l])
        F = jnp.cumsum(jnp.concatenate([cache_b_logf[l].astype(F32), lf], axis=1), axis=1)
        o_b = fox_attend(qb, cat(cache_b_k[l], kb), cat(cache_b_v[l], vb), F[:, P:], F, qpos, kpos_all)
        qc, kc, vc = c_inputs(p)
        bias, mask = band_bias_mask(c_bias[l], qpos, kpos_c)
        o_c = band_attend(qc, cat(cache_c_k[l], kc), cat(cache_c_v[l], vc), bias, mask)
        o_m = mem_attend(m_query(p), cache_mem_k[l], cache_mem_v[l])
        x = x + merge(p, o_a, o_b, o_c, o_m, w_out[l])
        sa_k.append(ka); sa_v.append(va); sa_ki.append(ki)
        sb_k.append(kb); sb_v.append(vb); sb_lf.append(lf)
        sc_k.append(kc); sc_v.append(vc)
    y_sample = rmsnorm(x, g_final)

    return (y_prompt, y_sample,
            jnp.stack(pa_k), jnp.stack(pa_v), jnp.stack(pa_ki),
            jnp.stack(pb_k), jnp.stack(pb_v), jnp.stack(pb_lf),
            jnp.stack(pc_k), jnp.stack(pc_v), jnp.stack(pm_k), jnp.stack(pm_v),
            jnp.stack(sa_k), jnp.stack(sa_v), jnp.stack(sa_ki),
            jnp.stack(sb_k), jnp.stack(sb_v), jnp.stack(sb_lf),
            jnp.stack(sc_k), jnp.stack(sc_v))
```

```python
import functools

import numpy as np
import jax
import jax.numpy as jnp
from jax import lax
from jax.experimental import pallas as pl
from jax.experimental.pallas import tpu as pltpu

F32 = jnp.float32
BF16 = jnp.bfloat16

D_MODEL = 1024
HEAD_DIM = 64
CHUNK = 64
N_HEADS = 4
W_BRANCH = N_HEADS * HEAD_DIM
ROT_DIM = HEAD_DIM // 4
ROPE_THETA = 500000.0
H_IDX = 4
D_IDX = HEAD_DIM
TOPK_MAX = 256
BAND_CHUNKS = 8
BAND = BAND_CHUNKS * CHUNK
REL_CLIP = 128
EPS = 1e-6

LANES = 128
NEG = -1e30
VMEM_LIMIT = 56 * 1024 * 1024

SLOT = 256
(S_AQ, S_AQI, S_MISC, S_AG, S_BQ, S_BK, S_BV, S_BG, S_CQ, S_CK, S_CV, S_CG, S_MQ, S_MG) = range(14)
N_SLOTS = 14
D_PAD = N_SLOTS * SLOT
M_AK, M_AV, M_AKI, M_WI, M_BF = 0, 64, 128, 192, 196

_IN_SPLITS = (('a_q', 256), ('a_k', 64), ('a_v', 64), ('a_qi', 256), ('a_ki', 64), ('a_wi', 4), ('a_g', 256),
              ('b_q', 256), ('b_k', 256), ('b_v', 256), ('b_f', 4), ('b_g', 256),
              ('c_q', 256), ('c_k', 256), ('c_v', 256), ('c_g', 256), ('m_q', 256), ('m_g', 256))
_DEST = {'a_q': S_AQ * SLOT, 'a_qi': S_AQI * SLOT, 'a_k': S_MISC * SLOT + M_AK, 'a_v': S_MISC * SLOT + M_AV,
         'a_ki': S_MISC * SLOT + M_AKI, 'a_wi': S_MISC * SLOT + M_WI, 'b_f': S_MISC * SLOT + M_BF,
         'a_g': S_AG * SLOT, 'b_q': S_BQ * SLOT, 'b_k': S_BK * SLOT, 'b_v': S_BV * SLOT, 'b_g': S_BG * SLOT,
         'c_q': S_CQ * SLOT, 'c_k': S_CK * SLOT, 'c_v': S_CV * SLOT, 'c_g': S_CG * SLOT,
         'm_q': S_MQ * SLOT, 'm_g': S_MG * SLOT}


def _column_map():
    src = np.zeros((D_PAD,), np.int32)
    valid = np.zeros((D_PAD,), bool)
    off = 0
    for name, size in _IN_SPLITS:
        d = _DEST[name]
        src[d:d + size] = np.arange(off, off + size)
        valid[d:d + size] = True
        off += size
    return src, valid


_COL_SRC, _COL_VALID = _column_map()

_INT_MIN = np.int32(-2 ** 31)
_NEG_INF_KEY = np.int32(np.array(-np.inf, np.float32).view(np.int32) ^ np.int32(0x7FFFFFFF))


def _cparams(sem):
    return pltpu.CompilerParams(dimension_semantics=sem, vmem_limit_bytes=VMEM_LIMIT)


def _rope(x, c, sa, sb):
    return x * c + pltpu.roll(x, 8, 1) * sa + pltpu.roll(x, LANES - 8, 1) * sb


def _proj_kernel(x_ref, g_ref, w_ref, c_ref, sa_ref, sb_ref, bf_ref, p_ref):
    x = x_ref[...]
    ms = jnp.mean(x * x, axis=-1, keepdims=True)
    h = ((x * lax.rsqrt(ms + EPS)) * g_ref[...]).astype(BF16)
    c, sa, sb = c_ref[...], sa_ref[...], sb_ref[...]
    lane = lax.broadcasted_iota(jnp.int32, c.shape, 1)
    first_head = lane < HEAD_DIM
    c1 = jnp.where(first_head, c, 1.0)
    sa1 = jnp.where(first_head, sa, 0.0)
    sb1 = jnp.where(first_head, sb, 0.0)
    for s in range(N_SLOTS):
        p = jnp.dot(h, w_ref[:, s * SLOT:(s + 1) * SLOT], preferred_element_type=F32)
        if s in (S_AQ, S_AQI):
            p = jnp.concatenate([_rope(p[:, :LANES], c, sa, sb), _rope(p[:, LANES:], c, sa, sb)], axis=1)
        elif s == S_MISC:
            lo = _rope(p[:, :LANES], c1, sa1, sb1)
            hi = p[:, LANES:]
            z = hi + bf_ref[...]
            lf = jnp.minimum(z, 0.0) - jnp.log1p(jnp.exp(-jnp.abs(z)))
            is_f = (lane >= M_BF - LANES) & (lane < M_BF - LANES + N_HEADS)
            hi = jnp.where(is_f, lf, _rope(hi, c1, sa1, sb1))
            p = jnp.concatenate([lo, hi], axis=1)
        p_ref[:, s * SLOT:(s + 1) * SLOT] = p


def _project(x2, g, w, tabs, bf_row, tm):
    n = x2.shape[0]
    t_rows = tabs[0].shape[0]
    nt = t_rows // tm
    tab_spec = pl.BlockSpec((tm, LANES), lambda i: (i % nt, 0))
    return pl.pallas_call(
        _proj_kernel,
        grid=(n // tm,),
        in_specs=[pl.BlockSpec((tm, D_MODEL), lambda i: (i, 0)),
                  pl.BlockSpec((1, D_MODEL), lambda i: (0, 0)),
                  pl.BlockSpec((D_MODEL, D_PAD), lambda i: (0, 0)),
                  tab_spec, tab_spec, tab_spec,
                  pl.BlockSpec((1, LANES), lambda i: (0, 0))],
        out_specs=pl.BlockSpec((tm, D_PAD), lambda i: (i, 0)),
        out_shape=jax.ShapeDtypeStruct((n, D_PAD), F32),
        compiler_params=_cparams(("parallel",)),
        name="proj",
    )(x2, g, w, *tabs, bf_row)


def _rope_tables(pos):
    half = ROT_DIM // 2
    inv = ROPE_THETA ** (-jnp.arange(half, dtype=F32) / half)
    ang = pos.astype(F32)[:, None] * inv[None, :]
    cos, sin = jnp.cos(ang), jnp.sin(ang)
    t = pos.shape[0]
    zeros = lambda w: jnp.zeros((t, w), F32)
    c = jnp.concatenate([cos, cos, jnp.ones((t, HEAD_DIM - ROT_DIM), F32)], axis=1)
    sa = jnp.concatenate([zeros(half), sin, zeros(HEAD_DIM - ROT_DIM)], axis=1)
    sb = jnp.concatenate([-sin, zeros(HEAD_DIM - half)], axis=1)
    rep = lambda a: jnp.concatenate([a, a], axis=1)
    return rep(c), rep(sa), rep(sb)


def _norm_matmul_kernel(x_ref, g_ref, w_ref, o_ref):
    x = x_ref[...]
    ms = jnp.mean(x * x, axis=-1, keepdims=True)
    h = ((x * lax.rsqrt(ms + EPS)) * g_ref[...]).astype(BF16)
    o_ref[...] = jnp.dot(h, w_ref[...], preferred_element_type=F32)


def _norm_matmul(x2, g, w, tm):
    n, d = x2.shape
    dn = w.shape[1]
    return pl.pallas_call(
        _norm_matmul_kernel,
        grid=(n // tm,),
        in_specs=[pl.BlockSpec((tm, d), lambda i: (i, 0)),
                  pl.BlockSpec((1, d), lambda i: (0, 0)),
                  pl.BlockSpec((d, dn), lambda i: (0, 0))],
        out_specs=pl.BlockSpec((tm, dn), lambda i: (i, 0)),
        out_shape=jax.ShapeDtypeStruct((n, dn), F32),
        compiler_params=_cparams(("parallel",)),
        name="mem_kv",
    )(x2, g, w)


def _cumsum_kernel(x_ref, o_ref):
    x = x_ref[...]
    n = x.shape[1]
    lane = lax.broadcasted_iota(jnp.int32, x.shape, 1)
    shift = 1
    while shift < n:
        x = x + jnp.where(lane >= shift, pltpu.roll(x, shift, 1), 0.0)
        shift *= 2
    o_ref[...] = x


def _cumsum_rows(x):
    return pl.pallas_call(
        _cumsum_kernel,
        out_shape=jax.ShapeDtypeStruct(x.shape, F32),
        name="forget_cumsum",
    )(x)


def _heads_to_rows(x):
    return jnp.concatenate([x[:, h * HEAD_DIM:(h + 1) * HEAD_DIM] for h in range(N_HEADS)], axis=0)


def _dsa_kernel(qa_ref, qi_ref, misc_ref, k_ref, v_ref, ki_ref, tri_ref, o_ref, keys_scr,
                *, tq, kc, l_true, pos0, topk):
    i = pl.program_id(1)
    q_first = pos0 + i * tq
    lim_max = jnp.minimum(((q_first + tq - 1) // CHUNK + 1) * CHUNK, l_true)
    nk = (lim_max + kc - 1) // kc
    row_pos = q_first + lax.broadcasted_iota(jnp.int32, (tq, 1), 0)
    lim = jnp.minimum((row_pos // CHUNK + 1) * CHUNK, l_true)

    qi = _heads_to_rows(qi_ref[...]).astype(BF16)
    wi = misc_ref[:, M_WI:M_WI + H_IDX] * (H_IDX ** -0.5 * D_IDX ** -0.5)

    def score_body(c, carry):
        kic = ki_ref[pl.ds(pl.multiple_of(c * kc, kc), kc), :].astype(BF16)
        dots = lax.dot_general(qi, kic, (((1,), (1,)), ((), ())), preferred_element_type=F32)
        score = None
        for h in range(H_IDX):
            r = jnp.maximum(dots[h * tq:(h + 1) * tq], 0.0) * wi[:, h:h + 1]
            score = r if score is None else score + r
        col = c * kc + lax.broadcasted_iota(jnp.int32, (tq, kc), 1)
        score = jnp.where(score == 0.0, 0.0, score)
        score = jnp.where(col < lim, score, -jnp.inf)
        bits = pltpu.bitcast(score, jnp.int32)
        keys_scr[c] = bits ^ ((bits >> 31) & jnp.int32(0x7FFFFFFF))
        return carry

    lax.fori_loop(0, nk, score_body, 0)

    def count(trial, strict):
        def body(c, acc):
            kch = keys_scr[c]
            m = (kch > trial) if strict else (kch >= trial)
            mf = jnp.where(m, 1.0, 0.0)
            part = mf[:, 0:LANES]
            for j in range(1, kc // LANES):
                part = part + mf[:, j * LANES:(j + 1) * LANES]
            return acc + part
        acc = lax.fori_loop(0, nk, body, jnp.zeros((tq, LANES), F32))
        return jnp.sum(acc, axis=1, keepdims=True)

    kf = jnp.float32(topk)

    def search_body(it, ans):
        trial_u = ans | lax.shift_left(jnp.int32(1), 31 - it)
        cnt = count(trial_u ^ _INT_MIN, False)
        return jnp.where(cnt >= kf, trial_u, ans)

    ans = lax.fori_loop(0, 32, search_body, jnp.zeros((tq, 1), jnp.int32))
    thr = ans ^ _INT_MIN
    need = kf - count(thr, True)

    qa = (_heads_to_rows(qa_ref[...]) * (HEAD_DIM ** -0.5)).astype(BF16)
    tri = tri_ref[...]

    def attn_body(c, carry):
        m_prev, l_prev, acc_prev, eq_seen = carry
        start = pl.multiple_of(c * kc, kc)
        kch = keys_scr[c]
        gt = kch > thr
        eq = kch == thr
        eqf = jnp.where(eq, 1.0, 0.0).astype(BF16)
        rank = jnp.dot(eqf, tri, preferred_element_type=F32) + eq_seen
        sel = (gt | (eq & (rank <= need))) & (kch != _NEG_INF_KEY)
        eq_seen = rank[:, kc - 1:kc]
        kk = k_ref[pl.ds(start, kc), :].astype(BF16)
        vv = v_ref[pl.ds(start, kc), :].astype(BF16)
        logits = lax.dot_general(qa, kk, (((1,), (1,)), ((), ())), preferred_element_type=F32)
        ms, ls, ps, alphas = [], [], [], []
        for h in range(N_HEADS):
            s = jnp.where(sel, logits[h * tq:(h + 1) * tq], NEG)
            m_new = jnp.maximum(m_prev[h], jnp.max(s, axis=1, keepdims=True))
            p = jnp.where(sel, jnp.exp(s - m_new), 0.0)
            alpha = jnp.exp(m_prev[h] - m_new)
            ms.append(m_new)
            ls.append(alpha * l_prev[h] + jnp.sum(p, axis=1, keepdims=True))
            ps.append(p.astype(BF16))
            alphas.append(alpha)
        pv = jnp.dot(jnp.concatenate(ps, axis=0), vv, preferred_element_type=F32)
        accs = [alphas[h] * acc_prev[h] + pv[h * tq:(h + 1) * tq] for h in range(N_HEADS)]
        return tuple(ms), tuple(ls), tuple(accs), eq_seen

    init = (tuple(jnp.full((tq, 1), NEG, F32) for _ in range(N_HEADS)),
            tuple(jnp.zeros((tq, 1), F32) for _ in range(N_HEADS)),
            tuple(jnp.zeros((tq, HEAD_DIM), F32) for _ in range(N_HEADS)),
            jnp.zeros((tq, 1), F32))
    _, l_fin, acc_fin, _ = lax.fori_loop(0, nk, attn_body, init)
    o_ref[...] = jnp.concatenate([acc_fin[h] / l_fin[h] for h in range(N_HEADS)], axis=1)


def _dsa(p3, k, v, ki, *, tq, kc, l_true, pos0, topk):
    b, t, _ = p3.shape
    lpad = k.shape[1]
    tri = (jnp.arange(kc)[:, None] <= jnp.arange(kc)[None, :]).astype(BF16)
    qspec = lambda slot: pl.BlockSpec((None, tq, SLOT), lambda bi, i, slot=slot: (bi, i, slot))
    kspec = pl.BlockSpec((None, lpad, HEAD_DIM), lambda bi, i: (bi, 0, 0))
    return pl.pallas_call(
        functools.partial(_dsa_kernel, tq=tq, kc=kc, l_true=l_true, pos0=pos0, topk=topk),
        grid=(b, t // tq),
        in_specs=[qspec(S_AQ), qspec(S_AQI), qspec(S_MISC), kspec, kspec, kspec,
                  pl.BlockSpec((kc, kc), lambda bi, i: (0, 0))],
        out_specs=pl.BlockSpec((None, tq, W_BRANCH), lambda bi, i: (bi, i, 0)),
        out_shape=jax.ShapeDtypeStruct((b, t, W_BRANCH), F32),
        scratch_shapes=[pltpu.VMEM((lpad // kc, tq, kc), jnp.int32)],
        compiler_params=_cparams(("parallel", "parallel")),
        name="dsa",
    )(p3, p3, p3, k, v, ki, tri)


def _fox_kernel(q_ref, k_ref, v_ref, fq_ref, fk_ref, o_ref, m_scr, l_scr, acc_scr, *, tq, tk, q_off, nkb):
    qb = pl.program_id(1)
    kb = pl.program_id(2)
    q_start = q_off + qb * tq
    k_start = kb * tk

    @pl.when(kb == 0)
    def _():
        m_scr[...] = jnp.full(m_scr.shape, NEG, F32)
        l_scr[...] = jnp.zeros(l_scr.shape, F32)
        acc_scr[...] = jnp.zeros(acc_scr.shape, F32)

    @pl.when(k_start <= q_start + tq - 1)
    def _():
        q = q_ref[...] * (HEAD_DIM ** -0.5)
        k = k_ref[...]
        v = v_ref[...]
        fq = fq_ref[...]
        fk = fk_ref[...]
        row = q_start + lax.broadcasted_iota(jnp.int32, (tq, tk), 0)
        col = k_start + lax.broadcasted_iota(jnp.int32, (tq, tk), 1)
        causal = col <= row
        for h in range(N_HEADS):
            sl = slice(h * HEAD_DIM, (h + 1) * HEAD_DIM)
            s = lax.dot_general(q[:, sl].astype(BF16), k[:, sl].astype(BF16), (((1,), (1,)), ((), ())),
                                preferred_element_type=F32)
            s = s + (fq[:, h:h + 1] - fk[h:h + 1, :])
            s = jnp.where(causal, s, NEG)
            m_prev = m_scr[h]
            m_new = jnp.maximum(m_prev, jnp.max(s, axis=1, keepdims=True))
            p = jnp.where(causal, jnp.exp(s - m_new), 0.0)
            alpha = jnp.exp(m_prev - m_new)
            l_scr[h] = alpha * l_scr[h] + jnp.sum(p, axis=1, keepdims=True)
            acc_scr[h] = alpha * acc_scr[h] + jnp.dot(p.astype(BF16), v[:, sl].astype(BF16),
                                                      preferred_element_type=F32)
            m_scr[h] = m_new

    @pl.when(kb == nkb - 1)
    def _():
        o_ref[...] = jnp.concatenate([acc_scr[h] / l_scr[h] for h in range(N_HEADS)], axis=1)


def _fox(q_arr, q_slot, k_arr, k_slot, v_arr, v_slot, f_col, f_row, *, tq, tk, q_off):
    b, t = q_arr.shape[:2]
    lpad = k_arr.shape[1]
    nkb = lpad // tk
    last_kb = lambda i: (q_off + i * tq + tq - 1) // tk
    return pl.pallas_call(
        functools.partial(_fox_kernel, tq=tq, tk=tk, q_off=q_off, nkb=nkb),
        grid=(b, t // tq, nkb),
        in_specs=[pl.BlockSpec((None, tq, SLOT), lambda bi, i, j: (bi, i, q_slot)),
                  pl.BlockSpec((None, tk, SLOT), lambda bi, i, j: (bi, jnp.minimum(j, last_kb(i)), k_slot)),
                  pl.BlockSpec((None, tk, SLOT), lambda bi, i, j: (bi, jnp.minimum(j, last_kb(i)), v_slot)),
                  pl.BlockSpec((None, tq, N_HEADS), lambda bi, i, j: (bi, i, 0)),
                  pl.BlockSpec((None, N_HEADS, tk), lambda bi, i, j: (bi, 0, jnp.minimum(j, last_kb(i))))],
        out_specs=pl.BlockSpec((None, tq, W_BRANCH), lambda bi, i, j: (bi, i, 0)),
        out_shape=jax.ShapeDtypeStruct((b, t, W_BRANCH), F32),
        scratch_shapes=[pltpu.VMEM((N_HEADS, tq, 1), F32), pltpu.VMEM((N_HEADS, tq, 1), F32),
                        pltpu.VMEM((N_HEADS, tq, HEAD_DIM), F32)],
        compiler_params=_cparams(("parallel", "parallel", "arbitrary")),
        name="fox",
    )(q_arr, k_arr, v_arr, f_col, f_row)


def _win_kernel(*refs, nparts, has_bias, tq, j_shift):
    q_ref = refs[0]
    k_refs = refs[1:1 + nparts]
    v_refs = refs[1 + nparts:1 + 2 * nparts]
    bias_ref = refs[1 + 2 * nparts] if has_bias else None
    o_ref = refs[-1]
    q = q_ref[...] * (HEAD_DIM ** -0.5)
    k = jnp.concatenate([r[...] for r in k_refs], axis=0) if nparts > 1 else k_refs[0][...]
    v = jnp.concatenate([r[...] for r in v_refs], axis=0) if nparts > 1 else v_refs[0][...]
    w = k.shape[0]
    outs = []
    if j_shift is not None:
        j_min = j_shift - pl.program_id(1) * tq
        live = lax.broadcasted_iota(jnp.int32, (tq, w), 1) >= j_min
    for h in range(N_HEADS):
        sl = slice(h * HEAD_DIM, (h + 1) * HEAD_DIM)
        s = lax.dot_general(q[:, sl].astype(BF16), k[:, sl].astype(BF16), (((1,), (1,)), ((), ())),
                            preferred_element_type=F32)
        if has_bias:
            s = s + bias_ref[h]
        if j_shift is not None:
            s = jnp.where(live, s, NEG)
        m = jnp.max(s, axis=1, keepdims=True)
        p = jnp.exp(s - m)
        l = jnp.sum(p, axis=1, keepdims=True)
        o = jnp.dot(p.astype(BF16), v[:, sl].astype(BF16), preferred_element_type=F32)
        outs.append(o / l)
    o_ref[...] = jnp.concatenate(outs, axis=1)


def _win_attention(q_arr, q_slot, kv_parts, bias, *, tq, j_shift):
    b, t = q_arr.shape[:2]
    nparts = len(kv_parts)
    in_specs = [pl.BlockSpec((None, tq, SLOT), lambda bi, i: (bi, i, q_slot))]
    args = [q_arr]
    for which in (0, 2):
        for part in kv_parts:
            arr, slot, rows, fn = part[which], part[which + 1], part[4], part[5]
            in_specs.append(pl.BlockSpec((None, rows, SLOT),
                                         lambda bi, i, slot=slot, fn=fn: (bi, fn(i), slot)))
            args.append(arr)
    if bias is not None:
        in_specs.append(pl.BlockSpec(bias.shape, lambda bi, i: (0, 0, 0)))
        args.append(bias)
    return pl.pallas_call(
        functools.partial(_win_kernel, nparts=nparts, has_bias=bias is not None, tq=tq, j_shift=j_shift),
        grid=(b, t // tq),
        in_specs=in_specs,
        out_specs=pl.BlockSpec((None, tq, W_BRANCH), lambda bi, i: (bi, i, 0)),
        out_shape=jax.ShapeDtypeStruct((b, t, W_BRANCH), F32),
        compiler_params=_cparams(("parallel", "parallel")),
        name="win_attn",
    )(*args)


def _band_bias(tab, qpos, kpos, valid_cols):
    rel = qpos[:, None] - kpos[None, :]
    idx = jnp.clip(rel, -REL_CLIP, REL_CLIP) + REL_CLIP
    qc = qpos[:, None] // CHUNK
    kc = kpos[None, :] // CHUNK
    mask = (kc <= qc) & (kc >= qc - BAND_CHUNKS) & valid_cols[None, :]
    return jnp.where(mask[None], tab.astype(F32)[:, idx], NEG)


def _merge_kernel(x_ref, oa_ref, ob_ref, oc_ref, om_ref, ga_ref, gb_ref, gc_ref, gm_ref, w_ref, gf_ref, y_ref,
                  *, final_norm):
    y = x_ref[...]
    pairs = ((oa_ref, ga_ref), (ob_ref, gb_ref), (oc_ref, gc_ref), (om_ref, gm_ref))
    for j, (o_ref, g_ref) in enumerate(pairs):
        g = g_ref[...]
        z = o_ref[...] * (g * (1.0 / (1.0 + jnp.exp(-g))))
        y = y + jnp.dot(z.astype(BF16), w_ref[j * W_BRANCH:(j + 1) * W_BRANCH, :], preferred_element_type=F32)
    if final_norm:
        ms = jnp.mean(y * y, axis=-1, keepdims=True)
        y = (y * lax.rsqrt(ms + EPS)) * gf_ref[...]
    y_ref[...] = y


def _merge(x2, outs, p2, w_out, g_final, tm, final_norm):
    n = x2.shape[0]
    o_spec = pl.BlockSpec((tm, W_BRANCH), lambda i: (i, 0))
    g_spec = lambda slot: pl.BlockSpec((tm, SLOT), lambda i, slot=slot: (i, slot))
    return pl.pallas_call(
        functools.partial(_merge_kernel, final_norm=final_norm),
        grid=(n // tm,),
        in_specs=[pl.BlockSpec((tm, D_MODEL), lambda i: (i, 0)), o_spec, o_spec, o_spec, o_spec,
                  g_spec(S_AG), g_spec(S_BG), g_spec(S_CG), g_spec(S_MG),
                  pl.BlockSpec((D_MODEL, D_MODEL), lambda i: (0, 0)),
                  pl.BlockSpec((1, D_MODEL), lambda i: (0, 0))],
        out_specs=pl.BlockSpec((tm, D_MODEL), lambda i: (i, 0)),
        out_shape=jax.ShapeDtypeStruct((n, D_MODEL), F32),
        compiler_params=_cparams(("parallel",)),
        name="merge",
    )(x2, *outs, p2, p2, p2, p2, w_out, g_final)


def _pad_rows(a, rows):
    return jnp.pad(a, ((0, 0), (0, rows - a.shape[1])) + ((0, 0),) * (a.ndim - 2))


def kernel(x_prompt, x_sample, cache_a_k, cache_a_v, cache_a_kidx, cache_b_k, cache_b_v, cache_b_logf,
           cache_c_k, cache_c_v, cache_mem_k, cache_mem_v, mem_prompt,
           w_in, w_out, g_norm, b_f, c_bias, g_mem, w_mem_kv, g_final):
    depth = w_in.shape[0]
    bsz, seq, _ = x_prompt.shape
    dbs, t_new, _ = x_sample.shape
    past = cache_a_k.shape[2]
    n_mem = mem_prompt.shape[1]
    c_keep = cache_c_k.shape[2]

    w_in_p = jnp.where(_COL_VALID[None, None, :], w_in[:, :, _COL_SRC], 0.0).astype(BF16)
    w_out_b = w_out.astype(BF16)
    w_mem_b = w_mem_kv.astype(BF16)
    bf_rows = jnp.zeros((depth, 1, LANES), F32).at[:, 0, M_BF - LANES:M_BF - LANES + N_HEADS].set(b_f.astype(F32))
    gf = g_final.reshape(1, D_MODEL)

    misc = lambda p3, off, w: p3[:, :, S_MISC * SLOT + off:S_MISC * SLOT + off + w]
    slot_of = lambda p3, s: p3[:, :, s * SLOT:(s + 1) * SLOT]

    tabs_p = _rope_tables(jnp.arange(seq))
    topk_p = min(TOPK_MAX, seq // 4)
    c_keep_p = min(BAND, seq)
    tq_band = 256
    w_band = BAND + tq_band
    rel_q = jnp.arange(tq_band)
    rel_k = jnp.arange(w_band) - BAND
    x = x_prompt.reshape(bsz * seq, D_MODEL)
    mem2 = mem_prompt.reshape(bsz * n_mem, D_MODEL)
    st = [[] for _ in range(10)]
    for l in range(depth):
        p2 = _project(x, g_norm[l].reshape(1, D_MODEL), w_in_p[l], tabs_p, bf_rows[l], 256)
        p3 = p2.reshape(bsz, seq, D_PAD)
        ka, va, kia = misc(p3, M_AK, 64), misc(p3, M_AV, 64), misc(p3, M_AKI, 64)
        lf = misc(p3, M_BF, N_HEADS)
        o_a = _dsa(p3, ka, va, kia, tq=128, kc=512, l_true=seq, pos0=0, topk=topk_p)
        f_row = _cumsum_rows(jnp.swapaxes(lf, 1, 2).reshape(bsz * N_HEADS, seq)).reshape(bsz, N_HEADS, seq)
        f_col = jnp.swapaxes(f_row, 1, 2)
        o_b = _fox(p3, S_BQ, p3, S_BK, p3, S_BV, f_col, f_row, tq=512, tk=512, q_off=0)
        bias = _band_bias(c_bias[l], rel_q, rel_k, jnp.ones((w_band,), bool))
        nprev = BAND // tq_band
        parts = [(p3, S_CK, p3, S_CV, tq_band, (lambda i, d=d: jnp.maximum(i - d, 0)))
                 for d in range(nprev, -1, -1)]
        o_c = _win_attention(p3, S_CQ, parts, bias, tq=tq_band, j_shift=BAND)
        kv = _norm_matmul(mem2, g_mem[l].reshape(1, D_MODEL), w_mem_b[l], 256).reshape(bsz, n_mem, 2 * W_BRANCH)
        o_m = _win_attention(p3, S_MQ, [(kv, 0, kv, 1, n_mem, lambda i: 0)], None, tq=512, j_shift=None)
        outs = [o.reshape(bsz * seq, W_BRANCH) for o in (o_a, o_b, o_c, o_m)]
        x = _merge(x, outs, p2, w_out_b[l], gf, 256, l == depth - 1)
        heads = lambda a: a.reshape(a.shape[0], a.shape[1], N_HEADS, HEAD_DIM)
        kc_all, vc_all = slot_of(p3, S_CK), slot_of(p3, S_CV)
        for lst, val in zip(st, (ka, va, kia, heads(slot_of(p3, S_BK)), heads(slot_of(p3, S_BV)), lf,
                                 heads(kc_all[:, seq - c_keep_p:]), heads(vc_all[:, seq - c_keep_p:]),
                                 heads(kv[:, :, :W_BRANCH]), heads(kv[:, :, W_BRANCH:]))):
            lst.append(val)
    y_prompt = x.reshape(bsz, seq, D_MODEL)
    prompt_state = [jnp.stack(s) for s in st]

    l_all = past + t_new
    qpos = past + jnp.arange(t_new)
    tabs_s = _rope_tables(jnp.tile(qpos, dbs))
    topk_s = min(TOPK_MAX, l_all // 4)
    kc_s = 512
    lpad_a = -(-l_all // kc_s) * kc_s
    lpad_b = -(-l_all // LANES) * LANES
    kpos_c = jnp.concatenate([past - c_keep + jnp.arange(c_keep), qpos])
    wc = c_keep + t_new
    wc_pad = -(-wc // LANES) * LANES
    kpos_c_pad = jnp.concatenate([kpos_c, jnp.zeros((wc_pad - wc,), kpos_c.dtype)])
    valid_c = jnp.arange(wc_pad) < wc
    flat4 = lambda a: a.reshape(a.shape[0], a.shape[1], W_BRANCH)
    x = x_sample.reshape(dbs * t_new, D_MODEL)
    ss = [[] for _ in range(8)]
    for l in range(depth):
        p2 = _project(x, g_norm[l].reshape(1, D_MODEL), w_in_p[l], tabs_s, bf_rows[l], dbs * t_new)
        p3 = p2.reshape(dbs, t_new, D_PAD)
        ka, va, kia = misc(p3, M_AK, 64), misc(p3, M_AV, 64), misc(p3, M_AKI, 64)
        lf = misc(p3, M_BF, N_HEADS)
        cat = lambda a, b_: jnp.concatenate([a, b_.astype(a.dtype)], axis=1)
        o_a = _dsa(p3, _pad_rows(cat(cache_a_k[l], ka), lpad_a), _pad_rows(cat(cache_a_v[l], va), lpad_a),
                   _pad_rows(cat(cache_a_kidx[l], kia), lpad_a),
                   tq=t_new, kc=kc_s, l_true=l_all, pos0=past, topk=topk_s)
        kb, vb = slot_of(p3, S_BK), slot_of(p3, S_BV)
        lf_all = jnp.concatenate([cache_b_logf[l].astype(F32), lf], axis=1)
        lf_rows = jnp.pad(jnp.swapaxes(lf_all, 1, 2).reshape(dbs * N_HEADS, l_all), ((0, 0), (0, lpad_b - l_all)))
        f_row = _cumsum_rows(lf_rows).reshape(dbs, N_HEADS, lpad_b)
        f_col = jnp.swapaxes(f_row[:, :, past:l_all], 1, 2)
        o_b = _fox(p3, S_BQ, _pad_rows(cat(flat4(cache_b_k[l]), kb), lpad_b), 0,
                   _pad_rows(cat(flat4(cache_b_v[l]), vb), lpad_b), 0, f_col, f_row,
                   tq=t_new, tk=lpad_b, q_off=past)
        kc_new, vc_new = slot_of(p3, S_CK), slot_of(p3, S_CV)
        bias = _band_bias(c_bias[l], qpos, kpos_c_pad, valid_c)
        o_c = _win_attention(p3, S_CQ, [(_pad_rows(cat(flat4(cache_c_k[l]), kc_new), wc_pad), 0,
                                         _pad_rows(cat(flat4(cache_c_v[l]), vc_new), wc_pad), 0,
                                         wc_pad, lambda i: 0)], bias, tq=t_new, j_shift=None)
        o_m = _win_attention(p3, S_MQ, [(flat4(cache_mem_k[l]), 0, flat4(cache_mem_v[l]), 0, n_mem, lambda i: 0)],
                             None, tq=t_new, j_shift=None)
        outs = [o.reshape(dbs * t_new, W_BRANCH) for o in (o_a, o_b, o_c, o_m)]
        x = _merge(x, outs, p2, w_out_b[l], gf, dbs * t_new, l == depth - 1)
        heads = lambda a: a.reshape(a.shape[0], a.shape[1], N_HEADS, HEAD_DIM)
        for lst, val in zip(ss, (ka, va, kia, heads(kb), heads(vb), lf, heads(kc_new), heads(vc_new))):
            lst.append(val)
    y_sample = x.reshape(dbs, t_new, D_MODEL)
    sample_state = [jnp.stack(s) for s in ss]

    return (y_prompt, y_sample, *prompt_state, *sample_state)
```

```python
import functools

import numpy as np
import jax
import jax.numpy as jnp
from jax import lax
from jax.experimental import pallas as pl
from jax.experimental.pallas import tpu as pltpu

F32 = jnp.float32
BF16 = jnp.bfloat16

D_MODEL = 1024
HEAD_DIM = 64
CHUNK = 64
N_HEADS = 4
W_BRANCH = N_HEADS * HEAD_DIM
ROT_DIM = HEAD_DIM // 4
ROPE_THETA = 500000.0
H_IDX = 4
D_IDX = HEAD_DIM
TOPK_MAX = 256
BAND_CHUNKS = 8
BAND = BAND_CHUNKS * CHUNK
REL_CLIP = 128
EPS = 1e-6

LANES = 128
NEG = -1e30
VMEM_LIMIT = 56 * 1024 * 1024

SLOT = 256
(S_AQ, S_AQI, S_MISC, S_AG, S_BQ, S_BK, S_BV, S_BG, S_CQ, S_CK, S_CV, S_CG, S_MQ, S_MG) = range(14)
N_SLOTS = 14
D_PAD = N_SLOTS * SLOT
M_AK, M_AV, M_AKI, M_WI, M_BF = 0, 64, 128, 192, 196

_IN_SPLITS = (('a_q', 256), ('a_k', 64), ('a_v', 64), ('a_qi', 256), ('a_ki', 64), ('a_wi', 4), ('a_g', 256),
              ('b_q', 256), ('b_k', 256), ('b_v', 256), ('b_f', 4), ('b_g', 256),
              ('c_q', 256), ('c_k', 256), ('c_v', 256), ('c_g', 256), ('m_q', 256), ('m_g', 256))
_DEST = {'a_q': S_AQ * SLOT, 'a_qi': S_AQI * SLOT, 'a_k': S_MISC * SLOT + M_AK, 'a_v': S_MISC * SLOT + M_AV,
         'a_ki': S_MISC * SLOT + M_AKI, 'a_wi': S_MISC * SLOT + M_WI, 'b_f': S_MISC * SLOT + M_BF,
         'a_g': S_AG * SLOT, 'b_q': S_BQ * SLOT, 'b_k': S_BK * SLOT, 'b_v': S_BV * SLOT, 'b_g': S_BG * SLOT,
         'c_q': S_CQ * SLOT, 'c_k': S_CK * SLOT, 'c_v': S_CV * SLOT, 'c_g': S_CG * SLOT,
         'm_q': S_MQ * SLOT, 'm_g': S_MG * SLOT}


def _pad_in_proj(w):
    src, off = {}, 0
    for name, size in _IN_SPLITS:
        src[name] = (off, size)
        off += size
    pieces, at = [], 0
    for name in sorted(_DEST, key=_DEST.get):
        if _DEST[name] > at:
            pieces.append(jnp.zeros(w.shape[:-1] + (_DEST[name] - at,), w.dtype))
        o, size = src[name]
        pieces.append(w[..., o:o + size])
        at = _DEST[name] + size
    if at < D_PAD:
        pieces.append(jnp.zeros(w.shape[:-1] + (D_PAD - at,), w.dtype))
    return jnp.concatenate(pieces, axis=-1)

_INT_MIN = np.int32(-2 ** 31)
_NEG_INF_KEY = np.int32(np.array(-np.inf, np.float32).view(np.int32) ^ np.int32(0x7FFFFFFF))


def _cparams(sem):
    return pltpu.CompilerParams(dimension_semantics=sem, vmem_limit_bytes=VMEM_LIMIT)


def _rope(x, c, sa, sb):
    return x * c + pltpu.roll(x, 8, 1) * sa + pltpu.roll(x, LANES - 8, 1) * sb


def _proj_kernel(x_ref, g_ref, w_ref, c_ref, sa_ref, sb_ref, bf_ref, p_ref):
    x = x_ref[...]
    ms = jnp.mean(x * x, axis=-1, keepdims=True)
    h = ((x * lax.rsqrt(ms + EPS)) * g_ref[...]).astype(BF16)
    c, sa, sb = c_ref[...], sa_ref[...], sb_ref[...]
    lane = lax.broadcasted_iota(jnp.int32, c.shape, 1)
    first_head = lane < HEAD_DIM
    c1 = jnp.where(first_head, c, 1.0)
    sa1 = jnp.where(first_head, sa, 0.0)
    sb1 = jnp.where(first_head, sb, 0.0)
    for s in range(N_SLOTS):
        p = jnp.dot(h, w_ref[:, s * SLOT:(s + 1) * SLOT], preferred_element_type=F32)
        if s in (S_AQ, S_AQI):
            p = jnp.concatenate([_rope(p[:, :LANES], c, sa, sb), _rope(p[:, LANES:], c, sa, sb)], axis=1)
        elif s == S_MISC:
            lo = _rope(p[:, :LANES], c1, sa1, sb1)
            hi = p[:, LANES:]
            z = hi + bf_ref[...]
            lf = jnp.minimum(z, 0.0) - jnp.log1p(jnp.exp(-jnp.abs(z)))
            is_f = (lane >= M_BF - LANES) & (lane < M_BF - LANES + N_HEADS)
            hi = jnp.where(is_f, lf, _rope(hi, c1, sa1, sb1))
            p = jnp.concatenate([lo, hi], axis=1)
        p_ref[:, s * SLOT:(s + 1) * SLOT] = p


def _project(x2, g, w, tabs, bf_row, tm):
    n = x2.shape[0]
    t_rows = tabs[0].shape[0]
    nt = t_rows // tm
    tab_spec = pl.BlockSpec((tm, LANES), lambda i: (i % nt, 0))
    return pl.pallas_call(
        _proj_kernel,
        grid=(n // tm,),
        in_specs=[pl.BlockSpec((tm, D_MODEL), lambda i: (i, 0)),
                  pl.BlockSpec((1, D_MODEL), lambda i: (0, 0)),
                  pl.BlockSpec((D_MODEL, D_PAD), lambda i: (0, 0)),
                  tab_spec, tab_spec, tab_spec,
                  pl.BlockSpec((1, LANES), lambda i: (0, 0))],
        out_specs=pl.BlockSpec((tm, D_PAD), lambda i: (i, 0)),
        out_shape=jax.ShapeDtypeStruct((n, D_PAD), F32),
        compiler_params=_cparams(("parallel",)),
        name="proj",
    )(x2, g, w, *tabs, bf_row)


def _rope_tables(pos):
    half = ROT_DIM // 2
    inv = ROPE_THETA ** (-jnp.arange(half, dtype=F32) / half)
    ang = pos.astype(F32)[:, None] * inv[None, :]
    cos, sin = jnp.cos(ang), jnp.sin(ang)
    t = pos.shape[0]
    zeros = lambda w: jnp.zeros((t, w), F32)
    c = jnp.concatenate([cos, cos, jnp.ones((t, HEAD_DIM - ROT_DIM), F32)], axis=1)
    sa = jnp.concatenate([zeros(half), sin, zeros(HEAD_DIM - ROT_DIM)], axis=1)
    sb = jnp.concatenate([-sin, zeros(HEAD_DIM - half)], axis=1)
    rep = lambda a: jnp.concatenate([a, a], axis=1)
    return rep(c), rep(sa), rep(sb)


def _norm_matmul_kernel(x_ref, g_ref, w_ref, o_ref):
    x = x_ref[...]
    ms = jnp.mean(x * x, axis=-1, keepdims=True)
    h = ((x * lax.rsqrt(ms + EPS)) * g_ref[...]).astype(BF16)
    o_ref[...] = jnp.dot(h, w_ref[...], preferred_element_type=F32)


def _norm_matmul(x2, g, w, tm):
    n, d = x2.shape
    dn = w.shape[1]
    return pl.pallas_call(
        _norm_matmul_kernel,
        grid=(n // tm,),
        in_specs=[pl.BlockSpec((tm, d), lambda i: (i, 0)),
                  pl.BlockSpec((1, d), lambda i: (0, 0)),
                  pl.BlockSpec((d, dn), lambda i: (0, 0))],
        out_specs=pl.BlockSpec((tm, dn), lambda i: (i, 0)),
        out_shape=jax.ShapeDtypeStruct((n, dn), F32),
        compiler_params=_cparams(("parallel",)),
        name="mem_kv",
    )(x2, g, w)


def _cumsum_kernel(x_ref, o_ref):
    x = x_ref[...]
    n = x.shape[1]
    lane = lax.broadcasted_iota(jnp.int32, x.shape, 1)
    shift = 1
    while shift < n:
        x = x + jnp.where(lane >= shift, pltpu.roll(x, shift, 1), 0.0)
        shift *= 2
    o_ref[...] = x


def _cumsum_rows(x):
    return pl.pallas_call(
        _cumsum_kernel,
        out_shape=jax.ShapeDtypeStruct(x.shape, F32),
        name="forget_cumsum",
    )(x)


def _heads_to_rows(x):
    return jnp.concatenate([x[:, h * HEAD_DIM:(h + 1) * HEAD_DIM] for h in range(N_HEADS)], axis=0)


def _dsa_kernel(qa_ref, qi_ref, misc_ref, k_ref, v_ref, ki_ref, tri_ref, o_ref, keys_scr,
                *, tq, kc, l_true, pos0, topk):
    i = pl.program_id(1)
    q_first = pos0 + i * tq
    lim_max = jnp.minimum(((q_first + tq - 1) // CHUNK + 1) * CHUNK, l_true)
    nk = (lim_max + kc - 1) // kc
    row_pos = q_first + lax.broadcasted_iota(jnp.int32, (tq, 1), 0)
    lim = jnp.minimum((row_pos // CHUNK + 1) * CHUNK, l_true)

    qi = _heads_to_rows(qi_ref[...]).astype(BF16)
    wi = misc_ref[:, M_WI:M_WI + H_IDX] * (H_IDX ** -0.5 * D_IDX ** -0.5)

    def score_body(c, carry):
        kic = ki_ref[pl.ds(pl.multiple_of(c * kc, kc), kc), :].astype(BF16)
        dots = lax.dot_general(qi, kic, (((1,), (1,)), ((), ())), preferred_element_type=F32)
        score = None
        for h in range(H_IDX):
            r = jnp.maximum(dots[h * tq:(h + 1) * tq], 0.0) * wi[:, h:h + 1]
            score = r if score is None else score + r
        col = c * kc + lax.broadcasted_iota(jnp.int32, (tq, kc), 1)
        score = jnp.where(score == 0.0, 0.0, score)
        score = jnp.where(col < lim, score, -jnp.inf)
        bits = pltpu.bitcast(score, jnp.int32)
        keys_scr[c] = bits ^ ((bits >> 31) & jnp.int32(0x7FFFFFFF))
        return carry

    lax.fori_loop(0, nk, score_body, 0)

    def count(trial, strict):
        def body(c, acc):
            kch = keys_scr[c]
            m = (kch > trial) if strict else (kch >= trial)
            mf = jnp.where(m, 1.0, 0.0)
            part = mf[:, 0:LANES]
            for j in range(1, kc // LANES):
                part = part + mf[:, j * LANES:(j + 1) * LANES]
            return acc + part
        acc = lax.fori_loop(0, nk, body, jnp.zeros((tq, LANES), F32))
        return jnp.sum(acc, axis=1, keepdims=True)

    kf = jnp.float32(topk)

    def search_body(it, ans):
        trial_u = ans | lax.shift_left(jnp.int32(1), 31 - it)
        cnt = count(trial_u ^ _INT_MIN, False)
        return jnp.where(cnt >= kf, trial_u, ans)

    ans = lax.fori_loop(0, 32, search_body, jnp.zeros((tq, 1), jnp.int32))
    thr = ans ^ _INT_MIN
    need = kf - count(thr, True)

    qa = (_heads_to_rows(qa_ref[...]) * (HEAD_DIM ** -0.5)).astype(BF16)
    tri = tri_ref[...]

    def attn_body(c, carry):
        m_prev, l_prev, acc_prev, eq_seen = carry
        start = pl.multiple_of(c * kc, kc)
        kch = keys_scr[c]
        gt = kch > thr
        eq = kch == thr
        eqf = jnp.where(eq, 1.0, 0.0).astype(BF16)
        rank = jnp.dot(eqf, tri, preferred_element_type=F32) + eq_seen
        sel = (gt | (eq & (rank <= need))) & (kch != _NEG_INF_KEY)
        eq_seen = rank[:, kc - 1:kc]
        kk = k_ref[pl.ds(start, kc), :].astype(BF16)
        vv = v_ref[pl.ds(start, kc), :].astype(BF16)
        logits = lax.dot_general(qa, kk, (((1,), (1,)), ((), ())), preferred_element_type=F32)
        ms, ls, ps, alphas = [], [], [], []
        for h in range(N_HEADS):
            s = jnp.where(sel, logits[h * tq:(h + 1) * tq], NEG)
            m_new = jnp.maximum(m_prev[h], jnp.max(s, axis=1, keepdims=True))
            p = jnp.where(sel, jnp.exp(s - m_new), 0.0)
            alpha = jnp.exp(m_prev[h] - m_new)
            ms.append(m_new)
            ls.append(alpha * l_prev[h] + jnp.sum(p, axis=1, keepdims=True))
            ps.append(p.astype(BF16))
            alphas.append(alpha)
        pv = jnp.dot(jnp.concatenate(ps, axis=0), vv, preferred_element_type=F32)
        accs = [alphas[h] * acc_prev[h] + pv[h * tq:(h + 1) * tq] for h in range(N_HEADS)]
        return tuple(ms), tuple(ls), tuple(accs), eq_seen

    init = (tuple(jnp.full((tq, 1), NEG, F32) for _ in range(N_HEADS)),
            tuple(jnp.zeros((tq, 1), F32) for _ in range(N_HEADS)),
            tuple(jnp.zeros((tq, HEAD_DIM), F32) for _ in range(N_HEADS)),
            jnp.zeros((tq, 1), F32))
    _, l_fin, acc_fin, _ = lax.fori_loop(0, nk, attn_body, init)
    o_ref[...] = jnp.concatenate([acc_fin[h] / l_fin[h] for h in range(N_HEADS)], axis=1)


def _dsa(p3, k, v, ki, *, tq, kc, l_true, pos0, topk):
    b, t, _ = p3.shape
    lpad = k.shape[1]
    tri = (jnp.arange(kc)[:, None] <= jnp.arange(kc)[None, :]).astype(BF16)
    qspec = lambda slot: pl.BlockSpec((None, tq, SLOT), lambda bi, i, slot=slot: (bi, i, slot))
    kspec = pl.BlockSpec((None, lpad, HEAD_DIM), lambda bi, i: (bi, 0, 0))
    return pl.pallas_call(
        functools.partial(_dsa_kernel, tq=tq, kc=kc, l_true=l_true, pos0=pos0, topk=topk),
        grid=(b, t // tq),
        in_specs=[qspec(S_AQ), qspec(S_AQI), qspec(S_MISC), kspec, kspec, kspec,
                  pl.BlockSpec((kc, kc), lambda bi, i: (0, 0))],
        out_specs=pl.BlockSpec((None, tq, W_BRANCH), lambda bi, i: (bi, i, 0)),
        out_shape=jax.ShapeDtypeStruct((b, t, W_BRANCH), F32),
        scratch_shapes=[pltpu.VMEM((lpad // kc, tq, kc), jnp.int32)],
        compiler_params=_cparams(("parallel", "parallel")),
        name="dsa",
    )(p3, p3, p3, k, v, ki, tri)


def _fold_rows(x, op):
    while x.shape[0] > 8:
        half = x.shape[0] // 2
        x = op(x[:half], x[half:])
    return x


def _dsa_t_kernel(qa_ref, qi_ref, misc_ref, k_ref, vt_ref, ki_ref, tril_ref, o_ref, keys_scr,
                  *, tq, kc, l_true, pos0, topk):
    i = pl.program_id(1)
    q_first = pos0 + i * tq
    lim_max = jnp.minimum(((q_first + tq - 1) // CHUNK + 1) * CHUNK, l_true)
    nk = (lim_max + kc - 1) // kc
    q_pos = q_first + lax.broadcasted_iota(jnp.int32, (1, tq), 1)
    lim = jnp.minimum((q_pos // CHUNK + 1) * CHUNK, l_true)

    qi = _heads_to_rows(qi_ref[...]).astype(BF16)
    misc_t = misc_ref[:, LANES:].T
    wi_t = misc_t[M_WI - LANES:M_WI - LANES + H_IDX, :] * (H_IDX ** -0.5 * D_IDX ** -0.5)
    key_row = lax.broadcasted_iota(jnp.int32, (kc, tq), 0)

    def score_body(c, carry):
        kic = ki_ref[pl.ds(pl.multiple_of(c * kc, kc), kc), :].astype(BF16)
        dots = lax.dot_general(kic, qi, (((1,), (1,)), ((), ())), preferred_element_type=F32)
        score = None
        for h in range(H_IDX):
            r = jnp.maximum(dots[:, h * tq:(h + 1) * tq], 0.0) * wi_t[h:h + 1, :]
            score = r if score is None else score + r
        score = jnp.where(score == 0.0, 0.0, score)
        score = jnp.where(key_row + c * kc < lim, score, -jnp.inf)
        bits = pltpu.bitcast(score, jnp.int32)
        keys_scr[c] = bits ^ ((bits >> 31) & jnp.int32(0x7FFFFFFF))
        return carry

    lax.fori_loop(0, nk, score_body, 0)

    def count(trial, strict):
        def body(c, acc):
            kch = keys_scr[c]
            m = (kch > trial) if strict else (kch >= trial)
            mf = jnp.where(m, 1.0, 0.0)
            return acc + _fold_rows(mf, jnp.add)
        acc = lax.fori_loop(0, nk, body, jnp.zeros((8, tq), F32))
        return jnp.sum(acc, axis=0, keepdims=True)

    kf = jnp.float32(topk)

    def search_body(it, ans):
        trial_u = ans | lax.shift_left(jnp.int32(1), 31 - it)
        cnt = count(trial_u ^ _INT_MIN, False)
        return jnp.where(cnt >= kf, trial_u, ans)

    ans = lax.fori_loop(0, 32, search_body, jnp.zeros((1, tq), jnp.int32))
    thr = ans ^ _INT_MIN
    need = jnp.where(thr == _NEG_INF_KEY, 0.0, kf - count(thr, True))

    qa = (_heads_to_rows(qa_ref[...]) * (HEAD_DIM ** -0.5)).astype(BF16)
    tril = tril_ref[...]

    def attn_body(c, carry):
        m_prev, l_prev, acc_prev, eq_seen = carry
        kch = keys_scr[c]
        eq = kch == thr
        rank = jnp.dot(tril, jnp.where(eq, 1.0, 0.0).astype(BF16), preferred_element_type=F32) + eq_seen
        bias = jnp.where(kch > thr, 0.0, jnp.where(eq, jnp.where(rank <= need, 0.0, NEG), NEG))
        eq_seen = rank[kc - 1:kc, :]
        kk = k_ref[pl.ds(pl.multiple_of(c * kc, kc), kc), :].astype(BF16)
        s = lax.dot_general(kk, qa, (((1,), (1,)), ((), ())), preferred_element_type=F32)
        s = s + jnp.concatenate([bias] * N_HEADS, axis=1)
        m_new = jnp.maximum(m_prev, jnp.max(_fold_rows(s, jnp.maximum), axis=0, keepdims=True))
        p = jnp.exp(s - m_new)
        alpha = jnp.exp(m_prev - m_new)
        l_new = alpha * l_prev + jnp.sum(_fold_rows(p, jnp.add), axis=0, keepdims=True)
        pv = jnp.dot(vt_ref[c], p.astype(BF16), preferred_element_type=F32)
        return m_new, l_new, alpha * acc_prev + pv, eq_seen

    init = (jnp.full((1, N_HEADS * tq), NEG, F32), jnp.zeros((1, N_HEADS * tq), F32),
            jnp.zeros((HEAD_DIM, N_HEADS * tq), F32), jnp.zeros((1, tq), F32))
    _, l_fin, acc_fin, _ = lax.fori_loop(0, nk, attn_body, init)
    o_t = acc_fin / l_fin
    o_ref[...] = jnp.concatenate([o_t[:, h * tq:(h + 1) * tq].T for h in range(N_HEADS)], axis=1)


def _dsa_t(p3, k, v, ki, *, tq, kc, l_true, pos0, topk):
    b, t, _ = p3.shape
    lpad = k.shape[1]
    nch = lpad // kc
    vt = jnp.swapaxes(v.astype(BF16).reshape(b, nch, kc, HEAD_DIM), 2, 3)
    tril = (jnp.arange(kc)[:, None] >= jnp.arange(kc)[None, :]).astype(BF16)
    qspec = lambda slot: pl.BlockSpec((None, tq, SLOT), lambda bi, i, slot=slot: (bi, i, slot))
    kspec = pl.BlockSpec((None, lpad, HEAD_DIM), lambda bi, i: (bi, 0, 0))
    return pl.pallas_call(
        functools.partial(_dsa_t_kernel, tq=tq, kc=kc, l_true=l_true, pos0=pos0, topk=topk),
        grid=(b, t // tq),
        in_specs=[qspec(S_AQ), qspec(S_AQI), qspec(S_MISC), kspec,
                  pl.BlockSpec((None, nch, HEAD_DIM, kc), lambda bi, i: (bi, 0, 0, 0)), kspec,
                  pl.BlockSpec((kc, kc), lambda bi, i: (0, 0))],
        out_specs=pl.BlockSpec((None, tq, W_BRANCH), lambda bi, i: (bi, i, 0)),
        out_shape=jax.ShapeDtypeStruct((b, t, W_BRANCH), F32),
        scratch_shapes=[pltpu.VMEM((nch, kc, tq), jnp.int32)],
        compiler_params=_cparams(("parallel", "parallel")),
        name="dsa_t",
    )(p3, p3, p3, k, vt, ki, tril)


def _fox_kernel(q_ref, k_ref, v_ref, fq_ref, fk_ref, o_ref, m_scr, l_scr, acc_scr, *, tq, tk, q_off, nkb):
    qb = pl.program_id(1)
    kb = pl.program_id(2)
    q_start = q_off + qb * tq
    k_start = kb * tk

    @pl.when(kb == 0)
    def _():
        m_scr[...] = jnp.full(m_scr.shape, NEG, F32)
        l_scr[...] = jnp.zeros(l_scr.shape, F32)
        acc_scr[...] = jnp.zeros(acc_scr.shape, F32)

    @pl.when(k_start <= q_start + tq - 1)
    def _():
        q = q_ref[...] * (HEAD_DIM ** -0.5)
        k = k_ref[...]
        v = v_ref[...]
        fq = fq_ref[...]
        fk = fk_ref[...]
        row = q_start + lax.broadcasted_iota(jnp.int32, (tq, tk), 0)
        col = k_start + lax.broadcasted_iota(jnp.int32, (tq, tk), 1)
        causal = col <= row
        for h in range(N_HEADS):
            sl = slice(h * HEAD_DIM, (h + 1) * HEAD_DIM)
            s = lax.dot_general(q[:, sl].astype(BF16), k[:, sl].astype(BF16), (((1,), (1,)), ((), ())),
                                preferred_element_type=F32)
            s = s + (fq[:, h:h + 1] - fk[h:h + 1, :])
            s = jnp.where(causal, s, NEG)
            m_prev = m_scr[h]
            m_new = jnp.maximum(m_prev, jnp.max(s, axis=1, keepdims=True))
            p = jnp.where(causal, jnp.exp(s - m_new), 0.0)
            alpha = jnp.exp(m_prev - m_new)
            l_scr[h] = alpha * l_scr[h] + jnp.sum(p, axis=1, keepdims=True)
            acc_scr[h] = alpha * acc_scr[h] + jnp.dot(p.astype(BF16), v[:, sl].astype(BF16),
                                                      preferred_element_type=F32)
            m_scr[h] = m_new

    @pl.when(kb == nkb - 1)
    def _():
        o_ref[...] = jnp.concatenate([acc_scr[h] / l_scr[h] for h in range(N_HEADS)], axis=1)


def _split3(x):
    hi = x.astype(BF16).astype(F32)
    r1 = x - hi
    mid = r1.astype(BF16).astype(F32)
    lo = (r1 - mid).astype(BF16).astype(F32)
    return hi, mid, lo


N_AUG = 3


def _fox_prep_kernel(q_ref, k_ref, v_ref, f_ref, qx_ref, kx_ref, vt_ref):
    q = q_ref[...] * (HEAD_DIM ** -0.5)
    k = k_ref[...]
    f = f_ref[...]
    ts = q.shape[0]
    lane = lax.broadcasted_iota(jnp.int32, (ts, HEAD_DIM), 1)
    vt = v_ref[...].T
    for h in range(N_HEADS):
        sl = slice(h * HEAD_DIM, (h + 1) * HEAD_DIM)
        f1, f2, f3 = _split3(f[:, h:h + 1])
        ext_q = jnp.where(lane == 0, f1, jnp.where(lane == 1, f2, jnp.where(lane == 2, f3,
                          jnp.where(lane < 2 * N_AUG, 1.0, 0.0))))
        ext_k = jnp.where(lane < N_AUG, 1.0, jnp.where(lane == 3, -f1, jnp.where(lane == 4, -f2,
                          jnp.where(lane == 5, -f3, 0.0))))
        qx_ref[h] = jnp.concatenate([q[:, sl], ext_q], axis=1).astype(BF16)
        kx_ref[h] = jnp.concatenate([k[:, sl], ext_k], axis=1).astype(BF16)
        vt_ref[h] = vt[sl, :].astype(BF16)


def _fox_t_kernel(qx_ref, kx_ref, vt_ref, o_ref, m_scr, l_scr, acc_scr, *, t, nkb):
    qb = pl.program_id(1)
    kb = pl.program_id(2)

    @pl.when(kb == 0)
    def _():
        m_scr[...] = jnp.full(m_scr.shape, NEG, F32)
        l_scr[...] = jnp.zeros(l_scr.shape, F32)
        acc_scr[...] = jnp.zeros(acc_scr.shape, F32)

    def step(diagonal):
        if diagonal:
            causal = (lax.broadcasted_iota(jnp.int32, (t, t), 0) <= lax.broadcasted_iota(jnp.int32, (t, t), 1))
        for h in range(N_HEADS):
            s = lax.dot_general(kx_ref[h], qx_ref[h], (((1,), (1,)), ((), ())), preferred_element_type=F32)
            if diagonal:
                s = jnp.where(causal, s, NEG)
            m_prev = m_scr[h]
            m_new = jnp.maximum(m_prev, jnp.max(_fold_rows(s, jnp.maximum), axis=0, keepdims=True))
            p = jnp.exp(s - m_new)
            alpha = jnp.exp(m_prev - m_new)
            l_scr[h] = alpha * l_scr[h] + jnp.sum(_fold_rows(p, jnp.add), axis=0, keepdims=True)
            acc_scr[h] = alpha * acc_scr[h] + jnp.dot(vt_ref[h], p.astype(BF16), preferred_element_type=F32)
            m_scr[h] = m_new

    pl.when(kb < qb)(lambda: step(False))
    pl.when(kb == qb)(lambda: step(True))

    @pl.when(kb == nkb - 1)
    def _():
        o_ref[...] = jnp.concatenate([(acc_scr[h] / l_scr[h]).T for h in range(N_HEADS)], axis=1)


def _fox_t(p3, f_col, *, t):
    b, s, _ = p3.shape
    n = s // t
    slot_spec = lambda slot: pl.BlockSpec((None, t, SLOT), lambda bi, i, slot=slot: (bi, i, slot))
    x_spec = pl.BlockSpec((None, N_HEADS, t, LANES), lambda bi, i: (bi, 0, i, 0))
    qx, kx, vt = pl.pallas_call(
        _fox_prep_kernel,
        grid=(b, n),
        in_specs=[slot_spec(S_BQ), slot_spec(S_BK), slot_spec(S_BV),
                  pl.BlockSpec((None, t, N_HEADS), lambda bi, i: (bi, i, 0))],
        out_specs=[x_spec, x_spec, pl.BlockSpec((None, N_HEADS, HEAD_DIM, t), lambda bi, i: (bi, 0, 0, i))],
        out_shape=[jax.ShapeDtypeStruct((b, N_HEADS, s, LANES), BF16),
                   jax.ShapeDtypeStruct((b, N_HEADS, s, LANES), BF16),
                   jax.ShapeDtypeStruct((b, N_HEADS, HEAD_DIM, s), BF16)],
        compiler_params=_cparams(("parallel", "parallel")),
        name="fox_prep",
    )(p3, p3, p3, f_col)
    return pl.pallas_call(
        functools.partial(_fox_t_kernel, t=t, nkb=n),
        grid=(b, n, n),
        in_specs=[pl.BlockSpec((None, N_HEADS, t, LANES), lambda bi, i, j: (bi, 0, i, 0)),
                  pl.BlockSpec((None, N_HEADS, t, LANES), lambda bi, i, j: (bi, 0, jnp.minimum(j, i), 0)),
                  pl.BlockSpec((None, N_HEADS, HEAD_DIM, t), lambda bi, i, j: (bi, 0, 0, jnp.minimum(j, i)))],
        out_specs=pl.BlockSpec((None, t, W_BRANCH), lambda bi, i, j: (bi, i, 0)),
        out_shape=jax.ShapeDtypeStruct((b, s, W_BRANCH), F32),
        scratch_shapes=[pltpu.VMEM((N_HEADS, 1, t), F32), pltpu.VMEM((N_HEADS, 1, t), F32),
                        pltpu.VMEM((N_HEADS, HEAD_DIM, t), F32)],
        compiler_params=_cparams(("parallel", "parallel", "arbitrary")),
        name="fox_t",
    )(qx, kx, vt)


def _fox(q_arr, q_slot, k_arr, k_slot, v_arr, v_slot, f_col, f_row, *, tq, tk, q_off):
    b, t = q_arr.shape[:2]
    lpad = k_arr.shape[1]
    nkb = lpad // tk
    last_kb = lambda i: (q_off + i * tq + tq - 1) // tk
    return pl.pallas_call(
        functools.partial(_fox_kernel, tq=tq, tk=tk, q_off=q_off, nkb=nkb),
        grid=(b, t // tq, nkb),
        in_specs=[pl.BlockSpec((None, tq, SLOT), lambda bi, i, j: (bi, i, q_slot)),
                  pl.BlockSpec((None, tk, SLOT), lambda bi, i, j: (bi, jnp.minimum(j, last_kb(i)), k_slot)),
                  pl.BlockSpec((None, tk, SLOT), lambda bi, i, j: (bi, jnp.minimum(j, last_kb(i)), v_slot)),
                  pl.BlockSpec((None, tq, N_HEADS), lambda bi, i, j: (bi, i, 0)),
                  pl.BlockSpec((None, N_HEADS, tk), lambda bi, i, j: (bi, 0, jnp.minimum(j, last_kb(i))))],
        out_specs=pl.BlockSpec((None, tq, W_BRANCH), lambda bi, i, j: (bi, i, 0)),
        out_shape=jax.ShapeDtypeStruct((b, t, W_BRANCH), F32),
        scratch_shapes=[pltpu.VMEM((N_HEADS, tq, 1), F32), pltpu.VMEM((N_HEADS, tq, 1), F32),
                        pltpu.VMEM((N_HEADS, tq, HEAD_DIM), F32)],
        compiler_params=_cparams(("parallel", "parallel", "arbitrary")),
        name="fox",
    )(q_arr, k_arr, v_arr, f_col, f_row)


def _win_kernel(*refs, nparts, has_bias, tq, j_shift):
    q_ref = refs[0]
    k_refs = refs[1:1 + nparts]
    v_refs = refs[1 + nparts:1 + 2 * nparts]
    bias_ref = refs[1 + 2 * nparts] if has_bias else None
    o_ref = refs[-1]
    q = q_ref[...] * (HEAD_DIM ** -0.5)
    k = jnp.concatenate([r[...] for r in k_refs], axis=0) if nparts > 1 else k_refs[0][...]
    v = jnp.concatenate([r[...] for r in v_refs], axis=0) if nparts > 1 else v_refs[0][...]
    w = k.shape[0]
    outs = []
    if j_shift is not None:
        j_min = j_shift - pl.program_id(1) * tq
        live = lax.broadcasted_iota(jnp.int32, (tq, w), 1) >= j_min
    for h in range(N_HEADS):
        sl = slice(h * HEAD_DIM, (h + 1) * HEAD_DIM)
        s = lax.dot_general(q[:, sl].astype(BF16), k[:, sl].astype(BF16), (((1,), (1,)), ((), ())),
                            preferred_element_type=F32)
        if has_bias:
            s = s + bias_ref[h]
        if j_shift is not None:
            s = jnp.where(live, s, NEG)
        m = jnp.max(s, axis=1, keepdims=True)
        p = jnp.exp(s - m)
        l = jnp.sum(p, axis=1, keepdims=True)
        o = jnp.dot(p.astype(BF16), v[:, sl].astype(BF16), preferred_element_type=F32)
        outs.append(o / l)
    o_ref[...] = jnp.concatenate(outs, axis=1)


def _win_attention(q_arr, q_slot, kv_parts, bias, *, tq, j_shift):
    b, t = q_arr.shape[:2]
    nparts = len(kv_parts)
    in_specs = [pl.BlockSpec((None, tq, SLOT), lambda bi, i: (bi, i, q_slot))]
    args = [q_arr]
    for which in (0, 2):
        for part in kv_parts:
            arr, slot, rows, fn = part[which], part[which + 1], part[4], part[5]
            in_specs.append(pl.BlockSpec((None, rows, SLOT),
                                         lambda bi, i, slot=slot, fn=fn: (bi, fn(i), slot)))
            args.append(arr)
    if bias is not None:
        in_specs.append(pl.BlockSpec(bias.shape, lambda bi, i: (0, 0, 0)))
        args.append(bias)
    return pl.pallas_call(
        functools.partial(_win_kernel, nparts=nparts, has_bias=bias is not None, tq=tq, j_shift=j_shift),
        grid=(b, t // tq),
        in_specs=in_specs,
        out_specs=pl.BlockSpec((None, tq, W_BRANCH), lambda bi, i: (bi, i, 0)),
        out_shape=jax.ShapeDtypeStruct((b, t, W_BRANCH), F32),
        compiler_params=_cparams(("parallel", "parallel")),
        name="win_attn",
    )(*args)


def _band_bias(tab, qpos, kpos, valid_cols):
    rel = qpos[:, None] - kpos[None, :]
    idx = jnp.clip(rel, -REL_CLIP, REL_CLIP) + REL_CLIP
    qc = qpos[:, None] // CHUNK
    kc = kpos[None, :] // CHUNK
    mask = (kc <= qc) & (kc >= qc - BAND_CHUNKS) & valid_cols[None, :]
    return jnp.where(mask[None], tab.astype(F32)[:, idx], NEG)


def _merge_kernel(x_ref, oa_ref, ob_ref, oc_ref, om_ref, ga_ref, gb_ref, gc_ref, gm_ref, w_ref, gf_ref, y_ref,
                  *, final_norm):
    y = x_ref[...]
    pairs = ((oa_ref, ga_ref), (ob_ref, gb_ref), (oc_ref, gc_ref), (om_ref, gm_ref))
    for j, (o_ref, g_ref) in enumerate(pairs):
        g = g_ref[...]
        z = o_ref[...] * (g * (1.0 / (1.0 + jnp.exp(-g))))
        y = y + jnp.dot(z.astype(BF16), w_ref[j * W_BRANCH:(j + 1) * W_BRANCH, :], preferred_element_type=F32)
    if final_norm:
        ms = jnp.mean(y * y, axis=-1, keepdims=True)
        y = (y * lax.rsqrt(ms + EPS)) * gf_ref[...]
    y_ref[...] = y


def _merge(x2, outs, p2, w_out, g_final, tm, final_norm):
    n = x2.shape[0]
    o_spec = pl.BlockSpec((tm, W_BRANCH), lambda i: (i, 0))
    g_spec = lambda slot: pl.BlockSpec((tm, SLOT), lambda i, slot=slot: (i, slot))
    return pl.pallas_call(
        functools.partial(_merge_kernel, final_norm=final_norm),
        grid=(n // tm,),
        in_specs=[pl.BlockSpec((tm, D_MODEL), lambda i: (i, 0)), o_spec, o_spec, o_spec, o_spec,
                  g_spec(S_AG), g_spec(S_BG), g_spec(S_CG), g_spec(S_MG),
                  pl.BlockSpec((D_MODEL, D_MODEL), lambda i: (0, 0)),
                  pl.BlockSpec((1, D_MODEL), lambda i: (0, 0))],
        out_specs=pl.BlockSpec((tm, D_MODEL), lambda i: (i, 0)),
        out_shape=jax.ShapeDtypeStruct((n, D_MODEL), F32),
        compiler_params=_cparams(("parallel",)),
        name="merge",
    )(x2, *outs, p2, p2, p2, p2, w_out, g_final)


def _pad_rows(a, rows):
    return jnp.pad(a, ((0, 0), (0, rows - a.shape[1])) + ((0, 0),) * (a.ndim - 2))


def kernel(x_prompt, x_sample, cache_a_k, cache_a_v, cache_a_kidx, cache_b_k, cache_b_v, cache_b_logf,
           cache_c_k, cache_c_v, cache_mem_k, cache_mem_v, mem_prompt,
           w_in, w_out, g_norm, b_f, c_bias, g_mem, w_mem_kv, g_final):
    depth = w_in.shape[0]
    bsz, seq, _ = x_prompt.shape
    dbs, t_new, _ = x_sample.shape
    past = cache_a_k.shape[2]
    n_mem = mem_prompt.shape[1]
    c_keep = cache_c_k.shape[2]

    w_in_p = _pad_in_proj(w_in.astype(BF16))
    w_out_b = w_out.astype(BF16)
    w_mem_b = w_mem_kv.astype(BF16)
    bf_rows = jnp.zeros((depth, 1, LANES), F32).at[:, 0, M_BF - LANES:M_BF - LANES + N_HEADS].set(b_f.astype(F32))
    gf = g_final.reshape(1, D_MODEL)

    misc = lambda p3, off, w: p3[:, :, S_MISC * SLOT + off:S_MISC * SLOT + off + w]
    slot_of = lambda p3, s: p3[:, :, s * SLOT:(s + 1) * SLOT]

    tabs_p = _rope_tables(jnp.arange(seq))
    topk_p = min(TOPK_MAX, seq // 4)
    c_keep_p = min(BAND, seq)
    tq_band = 256
    w_band = BAND + tq_band
    rel_q = jnp.arange(tq_band)
    rel_k = jnp.arange(w_band) - BAND
    x = x_prompt.reshape(bsz * seq, D_MODEL)
    mem2 = mem_prompt.reshape(bsz * n_mem, D_MODEL)
    st = [[] for _ in range(10)]
    for l in range(depth):
        p2 = _project(x, g_norm[l].reshape(1, D_MODEL), w_in_p[l], tabs_p, bf_rows[l], 256)
        p3 = p2.reshape(bsz, seq, D_PAD)
        ka, va, kia = misc(p3, M_AK, 64), misc(p3, M_AV, 64), misc(p3, M_AKI, 64)
        lf = misc(p3, M_BF, N_HEADS)
        o_a = _dsa_t(p3, ka, va, kia, tq=128, kc=512, l_true=seq, pos0=0, topk=topk_p)
        f_row = _cumsum_rows(jnp.swapaxes(lf, 1, 2).reshape(bsz * N_HEADS, seq)).reshape(bsz, N_HEADS, seq)
        f_col = jnp.swapaxes(f_row, 1, 2)
        o_b = _fox_t(p3, f_col, t=512)
        bias = _band_bias(c_bias[l], rel_q, rel_k, jnp.ones((w_band,), bool))
        nprev = BAND // tq_band
        parts = [(p3, S_CK, p3, S_CV, tq_band, (lambda i, d=d: jnp.maximum(i - d, 0)))
                 for d in range(nprev, -1, -1)]
        o_c = _win_attention(p3, S_CQ, parts, bias, tq=tq_band, j_shift=BAND)
        kv = _norm_matmul(mem2, g_mem[l].reshape(1, D_MODEL), w_mem_b[l], 256).reshape(bsz, n_mem, 2 * W_BRANCH)
        o_m = _win_attention(p3, S_MQ, [(kv, 0, kv, 1, n_mem, lambda i: 0)], None, tq=512, j_shift=None)
        outs = [o.reshape(bsz * seq, W_BRANCH) for o in (o_a, o_b, o_c, o_m)]
        x = _merge(x, outs, p2, w_out_b[l], gf, 256, l == depth - 1)
        heads = lambda a: a.reshape(a.shape[0], a.shape[1], N_HEADS, HEAD_DIM)
        kc_all, vc_all = slot_of(p3, S_CK), slot_of(p3, S_CV)
        for lst, val in zip(st, (ka, va, kia, heads(slot_of(p3, S_BK)), heads(slot_of(p3, S_BV)), lf,
                                 heads(kc_all[:, seq - c_keep_p:]), heads(vc_all[:, seq - c_keep_p:]),
                                 heads(kv[:, :, :W_BRANCH]), heads(kv[:, :, W_BRANCH:]))):
            lst.append(val)
    y_prompt = x.reshape(bsz, seq, D_MODEL)
    prompt_state = [jnp.stack(s) for s in st]

    l_all = past + t_new
    qpos = past + jnp.arange(t_new)
    tabs_s = _rope_tables(jnp.tile(qpos, dbs))
    topk_s = min(TOPK_MAX, l_all // 4)
    kc_s = 512
    lpad_a = -(-l_all // kc_s) * kc_s
    lpad_b = -(-l_all // LANES) * LANES
    kpos_c = jnp.concatenate([past - c_keep + jnp.arange(c_keep), qpos])
    wc = c_keep + t_new
    wc_pad = -(-wc // LANES) * LANES
    kpos_c_pad = jnp.concatenate([kpos_c, jnp.zeros((wc_pad - wc,), kpos_c.dtype)])
    valid_c = jnp.arange(wc_pad) < wc
    flat4 = lambda a: a.reshape(a.shape[0], a.shape[1], W_BRANCH)
    x = x_sample.reshape(dbs * t_new, D_MODEL)
    ss = [[] for _ in range(8)]
    for l in range(depth):
        p2 = _project(x, g_norm[l].reshape(1, D_MODEL), w_in_p[l], tabs_s, bf_rows[l], dbs * t_new)
        p3 = p2.reshape(dbs, t_new, D_PAD)
        ka, va, kia = misc(p3, M_AK, 64), misc(p3, M_AV, 64), misc(p3, M_AKI, 64)
        lf = misc(p3, M_BF, N_HEADS)
        cat = lambda a, b_: jnp.concatenate([a, b_.astype(a.dtype)], axis=1)
        o_a = _dsa(p3, _pad_rows(cat(cache_a_k[l], ka), lpad_a), _pad_rows(cat(cache_a_v[l], va), lpad_a),
                   _pad_rows(cat(cache_a_kidx[l], kia), lpad_a),
                   tq=t_new, kc=kc_s, l_true=l_all, pos0=past, topk=topk_s)
        kb, vb = slot_of(p3, S_BK), slot_of(p3, S_BV)
        lf_all = jnp.concatenate([cache_b_logf[l].astype(F32), lf], axis=1)
        lf_rows = jnp.pad(jnp.swapaxes(lf_all, 1, 2).reshape(dbs * N_HEADS, l_all), ((0, 0), (0, lpad_b - l_all)))
        f_row = _cumsum_rows(lf_rows).reshape(dbs, N_HEADS, lpad_b)
        f_col = jnp.swapaxes(f_row[:, :, past:l_all], 1, 2)
        o_b = _fox(p3, S_BQ, _pad_rows(cat(flat4(cache_b_k[l]), kb), lpad_b), 0,
                   _pad_rows(cat(flat4(cache_b_v[l]), vb), lpad_b), 0, f_col, f_row,
                   tq=t_new, tk=lpad_b, q_off=past)
        kc_new, vc_new = slot_of(p3, S_CK), slot_of(p3, S_CV)
        bias = _band_bias(c_bias[l], qpos, kpos_c_pad, valid_c)
        o_c = _win_attention(p3, S_CQ, [(_pad_rows(cat(flat4(cache_c_k[l]), kc_new), wc_pad), 0,
                                         _pad_rows(cat(flat4(cache_c_v[l]), vc_new), wc_pad), 0,
                                         wc_pad, lambda i: 0)], bias, tq=t_new, j_shift=None)
        o_m = _win_attention(p3, S_MQ, [(flat4(cache_mem_k[l]), 0, flat4(cache_mem_v[l]), 0, n_mem, lambda i: 0)],
                             None, tq=t_new, j_shift=None)
        outs = [o.reshape(dbs * t_new, W_BRANCH) for o in (o_a, o_b, o_c, o_m)]
        x = _merge(x, outs, p2, w_out_b[l], gf, dbs * t_new, l == depth - 1)
        heads = lambda a: a.reshape(a.shape[0], a.shape[1], N_HEADS, HEAD_DIM)
        for lst, val in zip(ss, (ka, va, kia, heads(kb), heads(vb), lf, heads(kc_new), heads(vc_new))):
            lst.append(val)
    y_sample = x.reshape(dbs, t_new, D_MODEL)
    sample_state = [jnp.stack(s) for s in ss]

    return (y_prompt, y_sample, *prompt_state, *sample_state)
```

```python
import functools

import numpy as np
import jax
import jax.numpy as jnp
from jax import lax
from jax.experimental import pallas as pl
from jax.experimental.pallas import tpu as pltpu

F32 = jnp.float32
BF16 = jnp.bfloat16

D_MODEL = 1024
HEAD_DIM = 64
CHUNK = 64
N_HEADS = 4
W_BRANCH = N_HEADS * HEAD_DIM
ROT_DIM = HEAD_DIM // 4
ROPE_THETA = 500000.0
H_IDX = 4
D_IDX = HEAD_DIM
TOPK_MAX = 256
BAND_CHUNKS = 8
BAND = BAND_CHUNKS * CHUNK
REL_CLIP = 128
EPS = 1e-6

LANES = 128
NEG = -1e30
VMEM_LIMIT = 56 * 1024 * 1024

SLOT = 256
(S_AQ, S_AQI, S_MISC, S_AG, S_BQ, S_BK, S_BV, S_BG, S_CQ, S_CK, S_CV, S_CG, S_MQ, S_MG) = range(14)
N_SLOTS = 14
D_PAD = N_SLOTS * SLOT
M_AK, M_AV, M_AKI, M_WI, M_BF = 0, 64, 128, 192, 196

_IN_SPLITS = (('a_q', 256), ('a_k', 64), ('a_v', 64), ('a_qi', 256), ('a_ki', 64), ('a_wi', 4), ('a_g', 256),
              ('b_q', 256), ('b_k', 256), ('b_v', 256), ('b_f', 4), ('b_g', 256),
              ('c_q', 256), ('c_k', 256), ('c_v', 256), ('c_g', 256), ('m_q', 256), ('m_g', 256))
_DEST = {'a_q': S_AQ * SLOT, 'a_qi': S_AQI * SLOT, 'a_k': S_MISC * SLOT + M_AK, 'a_v': S_MISC * SLOT + M_AV,
         'a_ki': S_MISC * SLOT + M_AKI, 'a_wi': S_MISC * SLOT + M_WI, 'b_f': S_MISC * SLOT + M_BF,
         'a_g': S_AG * SLOT, 'b_q': S_BQ * SLOT, 'b_k': S_BK * SLOT, 'b_v': S_BV * SLOT, 'b_g': S_BG * SLOT,
         'c_q': S_CQ * SLOT, 'c_k': S_CK * SLOT, 'c_v': S_CV * SLOT, 'c_g': S_CG * SLOT,
         'm_q': S_MQ * SLOT, 'm_g': S_MG * SLOT}


def _pad_in_proj(w):
    src, off = {}, 0
    for name, size in _IN_SPLITS:
        src[name] = (off, size)
        off += size
    pieces, at = [], 0
    for name in sorted(_DEST, key=_DEST.get):
        if _DEST[name] > at:
            pieces.append(jnp.zeros(w.shape[:-1] + (_DEST[name] - at,), w.dtype))
        o, size = src[name]
        pieces.append(w[..., o:o + size])
        at = _DEST[name] + size
    if at < D_PAD:
        pieces.append(jnp.zeros(w.shape[:-1] + (D_PAD - at,), w.dtype))
    return jnp.concatenate(pieces, axis=-1)

_INT_MIN = np.int32(-2 ** 31)
_NEG_INF_KEY = np.int32(np.array(-np.inf, np.float32).view(np.int32) ^ np.int32(0x7FFFFFFF))


def _cparams(sem):
    return pltpu.CompilerParams(dimension_semantics=sem, vmem_limit_bytes=VMEM_LIMIT)


def _rope(x, c, sa, sb):
    return x * c + pltpu.roll(x, 8, 1) * sa + pltpu.roll(x, LANES - 8, 1) * sb


def _proj_kernel(x_ref, g_ref, w_ref, c_ref, sa_ref, sb_ref, bf_ref, p_ref):
    x = x_ref[...]
    ms = jnp.mean(x * x, axis=-1, keepdims=True)
    h = ((x * lax.rsqrt(ms + EPS)) * g_ref[...]).astype(BF16)
    c, sa, sb = c_ref[...], sa_ref[...], sb_ref[...]
    lane = lax.broadcasted_iota(jnp.int32, c.shape, 1)
    first_head = lane < HEAD_DIM
    c1 = jnp.where(first_head, c, 1.0)
    sa1 = jnp.where(first_head, sa, 0.0)
    sb1 = jnp.where(first_head, sb, 0.0)
    for s in range(N_SLOTS):
        p = jnp.dot(h, w_ref[:, s * SLOT:(s + 1) * SLOT], preferred_element_type=F32)
        if s in (S_AQ, S_AQI):
            p = jnp.concatenate([_rope(p[:, :LANES], c, sa, sb), _rope(p[:, LANES:], c, sa, sb)], axis=1)
        elif s == S_MISC:
            lo = _rope(p[:, :LANES], c1, sa1, sb1)
            hi = p[:, LANES:]
            z = hi + bf_ref[...]
            lf = jnp.minimum(z, 0.0) - jnp.log1p(jnp.exp(-jnp.abs(z)))
            is_f = (lane >= M_BF - LANES) & (lane < M_BF - LANES + N_HEADS)
            hi = jnp.where(is_f, lf, _rope(hi, c1, sa1, sb1))
            p = jnp.concatenate([lo, hi], axis=1)
        p_ref[:, s * SLOT:(s + 1) * SLOT] = p


def _project(x2, g, w, tabs, bf_row, tm):
    n = x2.shape[0]
    t_rows = tabs[0].shape[0]
    nt = t_rows // tm
    tab_spec = pl.BlockSpec((tm, LANES), lambda i: (i % nt, 0))
    return pl.pallas_call(
        _proj_kernel,
        grid=(n // tm,),
        in_specs=[pl.BlockSpec((tm, D_MODEL), lambda i: (i, 0)),
                  pl.BlockSpec((1, D_MODEL), lambda i: (0, 0)),
                  pl.BlockSpec((D_MODEL, D_PAD), lambda i: (0, 0)),
                  tab_spec, tab_spec, tab_spec,
                  pl.BlockSpec((1, LANES), lambda i: (0, 0))],
        out_specs=pl.BlockSpec((tm, D_PAD), lambda i: (i, 0)),
        out_shape=jax.ShapeDtypeStruct((n, D_PAD), F32),
        compiler_params=_cparams(("parallel",)),
        name="proj",
    )(x2, g, w, *tabs, bf_row)


def _rope_tables(pos):
    half = ROT_DIM // 2
    inv = ROPE_THETA ** (-jnp.arange(half, dtype=F32) / half)
    ang = pos.astype(F32)[:, None] * inv[None, :]
    cos, sin = jnp.cos(ang), jnp.sin(ang)
    t = pos.shape[0]
    zeros = lambda w: jnp.zeros((t, w), F32)
    c = jnp.concatenate([cos, cos, jnp.ones((t, HEAD_DIM - ROT_DIM), F32)], axis=1)
    sa = jnp.concatenate([zeros(half), sin, zeros(HEAD_DIM - ROT_DIM)], axis=1)
    sb = jnp.concatenate([-sin, zeros(HEAD_DIM - half)], axis=1)
    rep = lambda a: jnp.concatenate([a, a], axis=1)
    return rep(c), rep(sa), rep(sb)


def _norm_matmul_kernel(x_ref, g_ref, w_ref, o_ref):
    x = x_ref[...]
    ms = jnp.mean(x * x, axis=-1, keepdims=True)
    h = ((x * lax.rsqrt(ms + EPS)) * g_ref[...]).astype(BF16)
    o_ref[...] = jnp.dot(h, w_ref[...], preferred_element_type=F32)


def _norm_matmul(x2, g, w, tm):
    n, d = x2.shape
    dn = w.shape[1]
    return pl.pallas_call(
        _norm_matmul_kernel,
        grid=(n // tm,),
        in_specs=[pl.BlockSpec((tm, d), lambda i: (i, 0)),
                  pl.BlockSpec((1, d), lambda i: (0, 0)),
                  pl.BlockSpec((d, dn), lambda i: (0, 0))],
        out_specs=pl.BlockSpec((tm, dn), lambda i: (i, 0)),
        out_shape=jax.ShapeDtypeStruct((n, dn), F32),
        compiler_params=_cparams(("parallel",)),
        name="mem_kv",
    )(x2, g, w)


def _cumsum_kernel(x_ref, o_ref):
    x = x_ref[...]
    n = x.shape[1]
    lane = lax.broadcasted_iota(jnp.int32, x.shape, 1)
    shift = 1
    while shift < n:
        x = x + jnp.where(lane >= shift, pltpu.roll(x, shift, 1), 0.0)
        shift *= 2
    o_ref[...] = x


def _cumsum_rows(x):
    return pl.pallas_call(
        _cumsum_kernel,
        out_shape=jax.ShapeDtypeStruct(x.shape, F32),
        name="forget_cumsum",
    )(x)


def _heads_to_rows(x):
    return jnp.concatenate([x[:, h * HEAD_DIM:(h + 1) * HEAD_DIM] for h in range(N_HEADS)], axis=0)


def _dsa_kernel(qa_ref, qi_ref, misc_ref, k_ref, v_ref, ki_ref, tri_ref, o_ref, keys_scr,
                *, tq, kc, l_true, pos0, topk):
    i = pl.program_id(1)
    q_first = pos0 + i * tq
    lim_max = jnp.minimum(((q_first + tq - 1) // CHUNK + 1) * CHUNK, l_true)
    nk = (lim_max + kc - 1) // kc
    row_pos = q_first + lax.broadcasted_iota(jnp.int32, (tq, 1), 0)
    lim = jnp.minimum((row_pos // CHUNK + 1) * CHUNK, l_true)

    qi = _heads_to_rows(qi_ref[...]).astype(BF16)
    wi = misc_ref[:, M_WI:M_WI + H_IDX] * (H_IDX ** -0.5 * D_IDX ** -0.5)

    def score_body(c, carry):
        kic = ki_ref[pl.ds(pl.multiple_of(c * kc, kc), kc), :].astype(BF16)
        dots = lax.dot_general(qi, kic, (((1,), (1,)), ((), ())), preferred_element_type=F32)
        score = None
        for h in range(H_IDX):
            r = jnp.maximum(dots[h * tq:(h + 1) * tq], 0.0) * wi[:, h:h + 1]
            score = r if score is None else score + r
        col = c * kc + lax.broadcasted_iota(jnp.int32, (tq, kc), 1)
        score = jnp.where(score == 0.0, 0.0, score)
        score = jnp.where(col < lim, score, -jnp.inf)
        bits = pltpu.bitcast(score, jnp.int32)
        keys_scr[c] = bits ^ ((bits >> 31) & jnp.int32(0x7FFFFFFF))
        return carry

    lax.fori_loop(0, nk, score_body, 0)

    def count(trial, strict):
        def body(c, acc):
            kch = keys_scr[c]
            m = (kch > trial) if strict else (kch >= trial)
            mf = jnp.where(m, 1.0, 0.0)
            part = mf[:, 0:LANES]
            for j in range(1, kc // LANES):
                part = part + mf[:, j * LANES:(j + 1) * LANES]
            return acc + part
        acc = lax.fori_loop(0, nk, body, jnp.zeros((tq, LANES), F32))
        return jnp.sum(acc, axis=1, keepdims=True)

    kf = jnp.float32(topk)

    def search_body(it, ans):
        trial_u = ans | lax.shift_left(jnp.int32(1), 31 - it)
        cnt = count(trial_u ^ _INT_MIN, False)
        return jnp.where(cnt >= kf, trial_u, ans)

    ans = lax.fori_loop(0, 32, search_body, jnp.zeros((tq, 1), jnp.int32))
    thr = ans ^ _INT_MIN
    need = kf - count(thr, True)

    qa = (_heads_to_rows(qa_ref[...]) * (HEAD_DIM ** -0.5)).astype(BF16)
    tri = tri_ref[...]

    def attn_body(c, carry):
        m_prev, l_prev, acc_prev, eq_seen = carry
        start = pl.multiple_of(c * kc, kc)
        kch = keys_scr[c]
        gt = kch > thr
        eq = kch == thr
        eqf = jnp.where(eq, 1.0, 0.0).astype(BF16)
        rank = jnp.dot(eqf, tri, preferred_element_type=F32) + eq_seen
        sel = (gt | (eq & (rank <= need))) & (kch != _NEG_INF_KEY)
        eq_seen = rank[:, kc - 1:kc]
        kk = k_ref[pl.ds(start, kc), :].astype(BF16)
        vv = v_ref[pl.ds(start, kc), :].astype(BF16)
        logits = lax.dot_general(qa, kk, (((1,), (1,)), ((), ())), preferred_element_type=F32)
        ms, ls, ps, alphas = [], [], [], []
        for h in range(N_HEADS):
            s = jnp.where(sel, logits[h * tq:(h + 1) * tq], NEG)
            m_new = jnp.maximum(m_prev[h], jnp.max(s, axis=1, keepdims=True))
            p = jnp.where(sel, jnp.exp(s - m_new), 0.0)
            alpha = jnp.exp(m_prev[h] - m_new)
            ms.append(m_new)
            ls.append(alpha * l_prev[h] + jnp.sum(p, axis=1, keepdims=True))
            ps.append(p.astype(BF16))
            alphas.append(alpha)
        pv = jnp.dot(jnp.concatenate(ps, axis=0), vv, preferred_element_type=F32)
        accs = [alphas[h] * acc_prev[h] + pv[h * tq:(h + 1) * tq] for h in range(N_HEADS)]
        return tuple(ms), tuple(ls), tuple(accs), eq_seen

    init = (tuple(jnp.full((tq, 1), NEG, F32) for _ in range(N_HEADS)),
            tuple(jnp.zeros((tq, 1), F32) for _ in range(N_HEADS)),
            tuple(jnp.zeros((tq, HEAD_DIM), F32) for _ in range(N_HEADS)),
            jnp.zeros((tq, 1), F32))
    _, l_fin, acc_fin, _ = lax.fori_loop(0, nk, attn_body, init)
    o_ref[...] = jnp.concatenate([acc_fin[h] / l_fin[h] for h in range(N_HEADS)], axis=1)


def _dsa(p3, k, v, ki, *, tq, kc, l_true, pos0, topk):
    b, t, _ = p3.shape
    lpad = k.shape[1]
    tri = (jnp.arange(kc)[:, None] <= jnp.arange(kc)[None, :]).astype(BF16)
    qspec = lambda slot: pl.BlockSpec((None, tq, SLOT), lambda bi, i, slot=slot: (bi, i, slot))
    kspec = pl.BlockSpec((None, lpad, HEAD_DIM), lambda bi, i: (bi, 0, 0))
    return pl.pallas_call(
        functools.partial(_dsa_kernel, tq=tq, kc=kc, l_true=l_true, pos0=pos0, topk=topk),
        grid=(b, t // tq),
        in_specs=[qspec(S_AQ), qspec(S_AQI), qspec(S_MISC), kspec, kspec, kspec,
                  pl.BlockSpec((kc, kc), lambda bi, i: (0, 0))],
        out_specs=pl.BlockSpec((None, tq, W_BRANCH), lambda bi, i: (bi, i, 0)),
        out_shape=jax.ShapeDtypeStruct((b, t, W_BRANCH), F32),
        scratch_shapes=[pltpu.VMEM((lpad // kc, tq, kc), jnp.int32)],
        compiler_params=_cparams(("parallel", "parallel")),
        name="dsa",
    )(p3, p3, p3, k, v, ki, tri)


def _fold_rows(x, op):
    while x.shape[0] > 8:
        half = x.shape[0] // 2
        x = op(x[:half], x[half:])
    return x


def _dsa_t_kernel(qa_ref, qi_ref, misc_ref, k_ref, vt_ref, ki_ref, tril_ref, o_ref, keys_scr,
                  *, tq, kc, l_true, pos0, topk):
    i = pl.program_id(1)
    q_first = pos0 + i * tq
    lim_max = jnp.minimum(((q_first + tq - 1) // CHUNK + 1) * CHUNK, l_true)
    nk = (lim_max + kc - 1) // kc
    q_pos = q_first + lax.broadcasted_iota(jnp.int32, (1, tq), 1)
    lim = jnp.minimum((q_pos // CHUNK + 1) * CHUNK, l_true)

    qi = _heads_to_rows(qi_ref[...]).astype(BF16)
    misc_t = misc_ref[:, LANES:].T
    wi_t = misc_t[M_WI - LANES:M_WI - LANES + H_IDX, :] * (H_IDX ** -0.5 * D_IDX ** -0.5)
    key_row = lax.broadcasted_iota(jnp.int32, (kc, tq), 0)

    def score_body(c, carry):
        kic = ki_ref[pl.ds(pl.multiple_of(c * kc, kc), kc), :].astype(BF16)
        dots = lax.dot_general(kic, qi, (((1,), (1,)), ((), ())), preferred_element_type=F32)
        score = None
        for h in range(H_IDX):
            r = jnp.maximum(dots[:, h * tq:(h + 1) * tq], 0.0) * wi_t[h:h + 1, :]
            score = r if score is None else score + r
        score = jnp.where(score == 0.0, 0.0, score)
        score = jnp.where(key_row + c * kc < lim, score, -jnp.inf)
        bits = pltpu.bitcast(score, jnp.int32)
        keys_scr[c] = bits ^ ((bits >> 31) & jnp.int32(0x7FFFFFFF))
        return carry

    lax.fori_loop(0, nk, score_body, 0)

    def count(trial, strict):
        def body(c, acc):
            kch = keys_scr[c]
            m = (kch > trial) if strict else (kch >= trial)
            mf = jnp.where(m, 1.0, 0.0)
            return acc + _fold_rows(mf, jnp.add)
        acc = lax.fori_loop(0, nk, body, jnp.zeros((8, tq), F32))
        return jnp.sum(acc, axis=0, keepdims=True)

    kf = jnp.float32(topk)

    def search_body(it, ans):
        trial_u = ans | lax.shift_left(jnp.int32(1), 31 - it)
        cnt = count(trial_u ^ _INT_MIN, False)
        return jnp.where(cnt >= kf, trial_u, ans)

    ans = lax.fori_loop(0, 32, search_body, jnp.zeros((1, tq), jnp.int32))
    thr = ans ^ _INT_MIN
    no_keys = thr == _NEG_INF_KEY
    n_gt = count(thr, True)
    need = jnp.where(no_keys, 0.0, kf - n_gt)
    all_ties_kept = jnp.all(no_keys | (count(thr, False) - n_gt == need))
    thr_all = jnp.where(no_keys, thr, thr - 1)

    qa = (_heads_to_rows(qa_ref[...]) * (HEAD_DIM ** -0.5)).astype(BF16)

    def softmax_step(c, bias, m_prev, l_prev, acc_prev):
        kk = k_ref[pl.ds(pl.multiple_of(c * kc, kc), kc), :].astype(BF16)
        s = lax.dot_general(kk, qa, (((1,), (1,)), ((), ())), preferred_element_type=F32)
        s = s + jnp.concatenate([bias] * N_HEADS, axis=1)
        m_new = jnp.maximum(m_prev, jnp.max(_fold_rows(s, jnp.maximum), axis=0, keepdims=True))
        p = jnp.exp(s - m_new)
        alpha = jnp.exp(m_prev - m_new)
        l_new = alpha * l_prev + jnp.sum(_fold_rows(p, jnp.add), axis=0, keepdims=True)
        pv = jnp.dot(vt_ref[c], p.astype(BF16), preferred_element_type=F32)
        return m_new, l_new, alpha * acc_prev + pv

    def plain_body(c, carry):
        bias = jnp.where(keys_scr[c] > thr_all, 0.0, NEG)
        return softmax_step(c, bias, *carry)

    def ranked_body(c, carry):
        m_prev, l_prev, acc_prev, eq_seen = carry
        kch = keys_scr[c]
        eq = kch == thr
        rank = jnp.dot(tril_ref[...], jnp.where(eq, 1.0, 0.0).astype(BF16), preferred_element_type=F32) + eq_seen
        bias = jnp.where(kch > thr, 0.0, jnp.where(eq, jnp.where(rank <= need, 0.0, NEG), NEG))
        return softmax_step(c, bias, m_prev, l_prev, acc_prev) + (rank[kc - 1:kc, :],)

    init = (jnp.full((1, N_HEADS * tq), NEG, F32), jnp.zeros((1, N_HEADS * tq), F32),
            jnp.zeros((HEAD_DIM, N_HEADS * tq), F32))
    _, l_fin, acc_fin = lax.cond(
        all_ties_kept,
        lambda: lax.fori_loop(0, nk, plain_body, init),
        lambda: lax.fori_loop(0, nk, ranked_body, init + (jnp.zeros((1, tq), F32),))[:3])
    o_t = acc_fin / l_fin
    o_ref[...] = jnp.concatenate([o_t[:, h * tq:(h + 1) * tq].T for h in range(N_HEADS)], axis=1)


def _dsa_t(p3, k, v, ki, *, tq, kc, l_true, pos0, topk):
    b, t, _ = p3.shape
    lpad = k.shape[1]
    nch = lpad // kc
    vt = jnp.swapaxes(v.astype(BF16).reshape(b, nch, kc, HEAD_DIM), 2, 3)
    tril = (jnp.arange(kc)[:, None] >= jnp.arange(kc)[None, :]).astype(BF16)
    qspec = lambda slot: pl.BlockSpec((None, tq, SLOT), lambda bi, i, slot=slot: (bi, i, slot))
    kspec = pl.BlockSpec((None, lpad, HEAD_DIM), lambda bi, i: (bi, 0, 0))
    return pl.pallas_call(
        functools.partial(_dsa_t_kernel, tq=tq, kc=kc, l_true=l_true, pos0=pos0, topk=topk),
        grid=(b, t // tq),
        in_specs=[qspec(S_AQ), qspec(S_AQI), qspec(S_MISC), kspec,
                  pl.BlockSpec((None, nch, HEAD_DIM, kc), lambda bi, i: (bi, 0, 0, 0)), kspec,
                  pl.BlockSpec((kc, kc), lambda bi, i: (0, 0))],
        out_specs=pl.BlockSpec((None, tq, W_BRANCH), lambda bi, i: (bi, i, 0)),
        out_shape=jax.ShapeDtypeStruct((b, t, W_BRANCH), F32),
        scratch_shapes=[pltpu.VMEM((nch, kc, tq), jnp.int32)],
        compiler_params=_cparams(("parallel", "parallel")),
        name="dsa_t",
    )(p3, p3, p3, k, vt, ki, tril)


def _fox_cached_kernel(q_ref, kc_ref, vc_ref, kn_ref, vn_ref, fq_ref, fkc_ref, fkn_ref, o_ref,
                       m_scr, l_scr, acc_scr, *, nkb):
    kb = pl.program_id(1)

    @pl.when(kb == 0)
    def _():
        m_scr[...] = jnp.full(m_scr.shape, NEG, F32)
        l_scr[...] = jnp.zeros(l_scr.shape, F32)
        acc_scr[...] = jnp.zeros(acc_scr.shape, F32)

    def step(k, v, fk, causal):
        q = q_ref[...] * (HEAD_DIM ** -0.5)
        fq = fq_ref[...]
        for h in range(N_HEADS):
            sl = slice(h * HEAD_DIM, (h + 1) * HEAD_DIM)
            s = lax.dot_general(q[:, sl].astype(BF16), k[:, sl].astype(BF16), (((1,), (1,)), ((), ())),
                                preferred_element_type=F32)
            s = s + (fq[:, h:h + 1] - fk[h:h + 1, :])
            if causal is not None:
                s = jnp.where(causal, s, NEG)
            m_prev = m_scr[h]
            m_new = jnp.maximum(m_prev, jnp.max(s, axis=1, keepdims=True))
            p = jnp.exp(s - m_new)
            alpha = jnp.exp(m_prev - m_new)
            l_scr[h] = alpha * l_scr[h] + jnp.sum(p, axis=1, keepdims=True)
            acc_scr[h] = alpha * acc_scr[h] + jnp.dot(p.astype(BF16), v[:, sl].astype(BF16),
                                                      preferred_element_type=F32)
            m_scr[h] = m_new

    @pl.when(kb < nkb)
    def _():
        step(kc_ref[...], vc_ref[...], fkc_ref[...], None)

    @pl.when(kb == nkb)
    def _():
        t = q_ref.shape[0]
        causal = lax.broadcasted_iota(jnp.int32, (t, t), 1) <= lax.broadcasted_iota(jnp.int32, (t, t), 0)
        step(kn_ref[...], vn_ref[...], fkn_ref[...], causal)
        o_ref[...] = jnp.concatenate([acc_scr[h] / l_scr[h] for h in range(N_HEADS)], axis=1)


def _split3(x):
    hi = x.astype(BF16).astype(F32)
    r1 = x - hi
    mid = r1.astype(BF16).astype(F32)
    lo = (r1 - mid).astype(BF16).astype(F32)
    return hi, mid, lo


N_AUG = 3


def _fox_prep_kernel(q_ref, k_ref, v_ref, f_ref, qx_ref, kx_ref, vt_ref):
    q = q_ref[...] * (HEAD_DIM ** -0.5)
    k = k_ref[...]
    f = f_ref[...]
    ts = q.shape[0]
    lane = lax.broadcasted_iota(jnp.int32, (ts, HEAD_DIM), 1)
    vt = v_ref[...].T
    for h in range(N_HEADS):
        sl = slice(h * HEAD_DIM, (h + 1) * HEAD_DIM)
        f1, f2, f3 = _split3(f[:, h:h + 1])
        ext_q = jnp.where(lane == 0, f1, jnp.where(lane == 1, f2, jnp.where(lane == 2, f3,
                          jnp.where(lane < 2 * N_AUG, 1.0, 0.0))))
        ext_k = jnp.where(lane < N_AUG, 1.0, jnp.where(lane == 3, -f1, jnp.where(lane == 4, -f2,
                          jnp.where(lane == 5, -f3, 0.0))))
        qx_ref[h] = jnp.concatenate([q[:, sl], ext_q], axis=1).astype(BF16)
        kx_ref[h] = jnp.concatenate([k[:, sl], ext_k], axis=1).astype(BF16)
        vt_ref[h] = vt[sl, :].astype(BF16)


def _fox_t_kernel(qx_ref, kx_ref, vt_ref, o_ref, m_scr, l_scr, acc_scr, *, t, nkb):
    qb = pl.program_id(1)
    kb = pl.program_id(2)

    @pl.when(kb == 0)
    def _():
        m_scr[...] = jnp.full(m_scr.shape, NEG, F32)
        l_scr[...] = jnp.zeros(l_scr.shape, F32)
        acc_scr[...] = jnp.zeros(acc_scr.shape, F32)

    def step(diagonal):
        if diagonal:
            causal = (lax.broadcasted_iota(jnp.int32, (t, t), 0) <= lax.broadcasted_iota(jnp.int32, (t, t), 1))
        for h in range(N_HEADS):
            s = lax.dot_general(kx_ref[h], qx_ref[h], (((1,), (1,)), ((), ())), preferred_element_type=F32)
            if diagonal:
                s = jnp.where(causal, s, NEG)
            m_prev = m_scr[h]
            m_new = jnp.maximum(m_prev, jnp.max(_fold_rows(s, jnp.maximum), axis=0, keepdims=True))
            p = jnp.exp(s - m_new)
            alpha = jnp.exp(m_prev - m_new)
            l_scr[h] = alpha * l_scr[h] + jnp.sum(_fold_rows(p, jnp.add), axis=0, keepdims=True)
            acc_scr[h] = alpha * acc_scr[h] + jnp.dot(vt_ref[h], p.astype(BF16), preferred_element_type=F32)
            m_scr[h] = m_new

    pl.when(kb < qb)(lambda: step(False))
    pl.when(kb == qb)(lambda: step(True))

    @pl.when(kb == nkb - 1)
    def _():
        o_ref[...] = jnp.concatenate([(acc_scr[h] / l_scr[h]).T for h in range(N_HEADS)], axis=1)


def _fox_t(p3, f_col, *, t):
    b, s, _ = p3.shape
    n = s // t
    slot_spec = lambda slot: pl.BlockSpec((None, t, SLOT), lambda bi, i, slot=slot: (bi, i, slot))
    x_spec = pl.BlockSpec((None, N_HEADS, t, LANES), lambda bi, i: (bi, 0, i, 0))
    qx, kx, vt = pl.pallas_call(
        _fox_prep_kernel,
        grid=(b, n),
        in_specs=[slot_spec(S_BQ), slot_spec(S_BK), slot_spec(S_BV),
                  pl.BlockSpec((None, t, N_HEADS), lambda bi, i: (bi, i, 0))],
        out_specs=[x_spec, x_spec, pl.BlockSpec((None, N_HEADS, HEAD_DIM, t), lambda bi, i: (bi, 0, 0, i))],
        out_shape=[jax.ShapeDtypeStruct((b, N_HEADS, s, LANES), BF16),
                   jax.ShapeDtypeStruct((b, N_HEADS, s, LANES), BF16),
                   jax.ShapeDtypeStruct((b, N_HEADS, HEAD_DIM, s), BF16)],
        compiler_params=_cparams(("parallel", "parallel")),
        name="fox_prep",
    )(p3, p3, p3, f_col)
    return pl.pallas_call(
        functools.partial(_fox_t_kernel, t=t, nkb=n),
        grid=(b, n, n),
        in_specs=[pl.BlockSpec((None, N_HEADS, t, LANES), lambda bi, i, j: (bi, 0, i, 0)),
                  pl.BlockSpec((None, N_HEADS, t, LANES), lambda bi, i, j: (bi, 0, jnp.minimum(j, i), 0)),
                  pl.BlockSpec((None, N_HEADS, HEAD_DIM, t), lambda bi, i, j: (bi, 0, 0, jnp.minimum(j, i)))],
        out_specs=pl.BlockSpec((None, t, W_BRANCH), lambda bi, i, j: (bi, i, 0)),
        out_shape=jax.ShapeDtypeStruct((b, s, W_BRANCH), F32),
        scratch_shapes=[pltpu.VMEM((N_HEADS, 1, t), F32), pltpu.VMEM((N_HEADS, 1, t), F32),
                        pltpu.VMEM((N_HEADS, HEAD_DIM, t), F32)],
        compiler_params=_cparams(("parallel", "parallel", "arbitrary")),
        name="fox_t",
    )(qx, kx, vt)


def _fox_cached(p3, cache_k, cache_v, layer, f_col, f_cache, f_new, *, tk):
    b, t, _ = p3.shape
    nkb = cache_k.shape[1] // tk
    cache_blk = lambda bi, j: (layer * b + bi, jnp.minimum(j, nkb - 1), 0)
    new_slot = lambda slot: pl.BlockSpec((None, t, SLOT), lambda bi, j, slot=slot: (bi, 0, slot))
    return pl.pallas_call(
        functools.partial(_fox_cached_kernel, nkb=nkb),
        grid=(b, nkb + 1),
        in_specs=[new_slot(S_BQ),
                  pl.BlockSpec((None, tk, W_BRANCH), cache_blk), pl.BlockSpec((None, tk, W_BRANCH), cache_blk),
                  new_slot(S_BK), new_slot(S_BV),
                  pl.BlockSpec((None, t, N_HEADS), lambda bi, j: (bi, 0, 0)),
                  pl.BlockSpec((None, N_HEADS, tk), lambda bi, j: (bi, 0, jnp.minimum(j, nkb - 1))),
                  pl.BlockSpec((None, N_HEADS, t), lambda bi, j: (bi, 0, 0))],
        out_specs=pl.BlockSpec((None, t, W_BRANCH), lambda bi, j: (bi, 0, 0)),
        out_shape=jax.ShapeDtypeStruct((b, t, W_BRANCH), F32),
        scratch_shapes=[pltpu.VMEM((N_HEADS, t, 1), F32), pltpu.VMEM((N_HEADS, t, 1), F32),
                        pltpu.VMEM((N_HEADS, t, HEAD_DIM), F32)],
        compiler_params=_cparams(("parallel", "arbitrary")),
        name="fox_cached",
    )(p3, cache_k, cache_v, p3, p3, f_col, f_cache, f_new)


def _win_kernel(*refs, nparts, has_bias, tq, j_shift):
    q_ref = refs[0]
    k_refs = refs[1:1 + nparts]
    v_refs = refs[1 + nparts:1 + 2 * nparts]
    bias_ref = refs[1 + 2 * nparts] if has_bias else None
    o_ref = refs[-1]
    q = q_ref[...] * (HEAD_DIM ** -0.5)
    k = jnp.concatenate([r[...] for r in k_refs], axis=0) if nparts > 1 else k_refs[0][...]
    v = jnp.concatenate([r[...] for r in v_refs], axis=0) if nparts > 1 else v_refs[0][...]
    w = k.shape[0]
    outs = []
    if j_shift is not None:
        j_min = j_shift - pl.program_id(1) * tq
        live = lax.broadcasted_iota(jnp.int32, (tq, w), 1) >= j_min
    for h in range(N_HEADS):
        sl = slice(h * HEAD_DIM, (h + 1) * HEAD_DIM)
        s = lax.dot_general(q[:, sl].astype(BF16), k[:, sl].astype(BF16), (((1,), (1,)), ((), ())),
                            preferred_element_type=F32)
        if has_bias:
            s = s + bias_ref[h]
        if j_shift is not None:
            s = jnp.where(live, s, NEG)
        m = jnp.max(s, axis=1, keepdims=True)
        p = jnp.exp(s - m)
        l = jnp.sum(p, axis=1, keepdims=True)
        o = jnp.dot(p.astype(BF16), v[:, sl].astype(BF16), preferred_element_type=F32)
        outs.append(o / l)
    o_ref[...] = jnp.concatenate(outs, axis=1)


def _win_attention(q_arr, q_slot, kv_parts, bias, *, tq, j_shift):
    b, t = q_arr.shape[:2]
    nparts = len(kv_parts)
    in_specs = [pl.BlockSpec((None, tq, SLOT), lambda bi, i: (bi, i, q_slot))]
    args = [q_arr]
    for which in (0, 2):
        for part in kv_parts:
            arr, slot, rows, fn, boff = part[which], part[which + 1], part[4], part[5], part[6]
            in_specs.append(pl.BlockSpec((None, rows, SLOT),
                                         lambda bi, i, slot=slot, fn=fn, boff=boff: (boff + bi, fn(i), slot)))
            args.append(arr)
    if bias is not None:
        in_specs.append(pl.BlockSpec(bias.shape, lambda bi, i: (0, 0, 0)))
        args.append(bias)
    return pl.pallas_call(
        functools.partial(_win_kernel, nparts=nparts, has_bias=bias is not None, tq=tq, j_shift=j_shift),
        grid=(b, t // tq),
        in_specs=in_specs,
        out_specs=pl.BlockSpec((None, tq, W_BRANCH), lambda bi, i: (bi, i, 0)),
        out_shape=jax.ShapeDtypeStruct((b, t, W_BRANCH), F32),
        compiler_params=_cparams(("parallel", "parallel")),
        name="win_attn",
    )(*args)


def _band_bias(tab, q0, k0, nq, nk, nk_valid):
    m = np.arange(nq + nk - 1)
    idx = np.clip((q0 - k0) + (nq - 1) - m, -REL_CLIP, REL_CLIP) + REL_CLIP
    r = jnp.pad(tab.astype(F32)[:, idx], ((0, 0), (0, 1)))
    skew = jnp.tile(r, (1, nq))[:, :nq * (nq + nk - 1)].reshape(tab.shape[0], nq, nq + nk - 1)
    bias = skew[:, :, nq - 1:nq - 1 + nk]
    qc = (q0 + np.arange(nq))[:, None] // CHUNK
    kc = (k0 + np.arange(nk))[None, :] // CHUNK
    mask = (kc <= qc) & (kc >= qc - BAND_CHUNKS) & (np.arange(nk) < nk_valid)[None, :]
    return jnp.where(mask[None], bias, NEG)


def _merge_kernel(x_ref, oa_ref, ob_ref, oc_ref, om_ref, ga_ref, gb_ref, gc_ref, gm_ref, w_ref, gf_ref, y_ref,
                  *, final_norm):
    y = x_ref[...]
    pairs = ((oa_ref, ga_ref), (ob_ref, gb_ref), (oc_ref, gc_ref), (om_ref, gm_ref))
    for j, (o_ref, g_ref) in enumerate(pairs):
        g = g_ref[...]
        z = o_ref[...] * (g * (1.0 / (1.0 + jnp.exp(-g))))
        y = y + jnp.dot(z.astype(BF16), w_ref[j * W_BRANCH:(j + 1) * W_BRANCH, :], preferred_element_type=F32)
    if final_norm:
        ms = jnp.mean(y * y, axis=-1, keepdims=True)
        y = (y * lax.rsqrt(ms + EPS)) * gf_ref[...]
    y_ref[...] = y


def _merge(x2, outs, p2, w_out, g_final, tm, final_norm):
    n = x2.shape[0]
    o_spec = pl.BlockSpec((tm, W_BRANCH), lambda i: (i, 0))
    g_spec = lambda slot: pl.BlockSpec((tm, SLOT), lambda i, slot=slot: (i, slot))
    return pl.pallas_call(
        functools.partial(_merge_kernel, final_norm=final_norm),
        grid=(n // tm,),
        in_specs=[pl.BlockSpec((tm, D_MODEL), lambda i: (i, 0)), o_spec, o_spec, o_spec, o_spec,
                  g_spec(S_AG), g_spec(S_BG), g_spec(S_CG), g_spec(S_MG),
                  pl.BlockSpec((D_MODEL, D_MODEL), lambda i: (0, 0)),
                  pl.BlockSpec((1, D_MODEL), lambda i: (0, 0))],
        out_specs=pl.BlockSpec((tm, D_MODEL), lambda i: (i, 0)),
        out_shape=jax.ShapeDtypeStruct((n, D_MODEL), F32),
        compiler_params=_cparams(("parallel",)),
        name="merge",
    )(x2, *outs, p2, p2, p2, p2, w_out, g_final)


def _pad_rows(a, rows):
    return jnp.pad(a, ((0, 0), (0, rows - a.shape[1])) + ((0, 0),) * (a.ndim - 2))


def kernel(x_prompt, x_sample, cache_a_k, cache_a_v, cache_a_kidx, cache_b_k, cache_b_v, cache_b_logf,
           cache_c_k, cache_c_v, cache_mem_k, cache_mem_v, mem_prompt,
           w_in, w_out, g_norm, b_f, c_bias, g_mem, w_mem_kv, g_final):
    depth = w_in.shape[0]
    bsz, seq, _ = x_prompt.shape
    dbs, t_new, _ = x_sample.shape
    past = cache_a_k.shape[2]
    n_mem = mem_prompt.shape[1]
    c_keep = cache_c_k.shape[2]

    w_in_p = _pad_in_proj(w_in.astype(BF16))
    w_out_b = w_out.astype(BF16)
    w_mem_b = w_mem_kv.astype(BF16)
    bf_rows = jnp.zeros((depth, 1, LANES), F32).at[:, 0, M_BF - LANES:M_BF - LANES + N_HEADS].set(b_f.astype(F32))
    gf = g_final.reshape(1, D_MODEL)

    misc = lambda p3, off, w: p3[:, :, S_MISC * SLOT + off:S_MISC * SLOT + off + w]
    slot_of = lambda p3, s: p3[:, :, s * SLOT:(s + 1) * SLOT]

    tabs_p = _rope_tables(jnp.arange(seq))
    topk_p = min(TOPK_MAX, seq // 4)
    c_keep_p = min(BAND, seq)
    tq_band = 256
    w_band = BAND + tq_band
    x = x_prompt.reshape(bsz * seq, D_MODEL)
    mem2 = mem_prompt.reshape(bsz * n_mem, D_MODEL)
    st = [[] for _ in range(10)]
    for l in range(depth):
        p2 = _project(x, g_norm[l].reshape(1, D_MODEL), w_in_p[l], tabs_p, bf_rows[l], 256)
        p3 = p2.reshape(bsz, seq, D_PAD)
        ka, va, kia = misc(p3, M_AK, 64), misc(p3, M_AV, 64), misc(p3, M_AKI, 64)
        lf = misc(p3, M_BF, N_HEADS)
        o_a = _dsa_t(p3, ka, va, kia, tq=128, kc=512, l_true=seq, pos0=0, topk=topk_p)
        f_row = _cumsum_rows(jnp.swapaxes(lf, 1, 2).reshape(bsz * N_HEADS, seq)).reshape(bsz, N_HEADS, seq)
        f_col = jnp.swapaxes(f_row, 1, 2)
        o_b = _fox_t(p3, f_col, t=512)
        bias = _band_bias(c_bias[l], 0, -BAND, tq_band, w_band, w_band)
        nprev = BAND // tq_band
        parts = [(p3, S_CK, p3, S_CV, tq_band, (lambda i, d=d: jnp.maximum(i - d, 0)), 0)
                 for d in range(nprev, -1, -1)]
        o_c = _win_attention(p3, S_CQ, parts, bias, tq=tq_band, j_shift=BAND)
        kv = _norm_matmul(mem2, g_mem[l].reshape(1, D_MODEL), w_mem_b[l], 256).reshape(bsz, n_mem, 2 * W_BRANCH)
        o_m = _win_attention(p3, S_MQ, [(kv, 0, kv, 1, n_mem, lambda i: 0, 0)], None, tq=512, j_shift=None)
        outs = [o.reshape(bsz * seq, W_BRANCH) for o in (o_a, o_b, o_c, o_m)]
        x = _merge(x, outs, p2, w_out_b[l], gf, 256, l == depth - 1)
        heads = lambda a: a.reshape(a.shape[0], a.shape[1], N_HEADS, HEAD_DIM)
        kc_all, vc_all = slot_of(p3, S_CK), slot_of(p3, S_CV)
        for lst, val in zip(st, (ka, va, kia, heads(slot_of(p3, S_BK)), heads(slot_of(p3, S_BV)), lf,
                                 heads(kc_all[:, seq - c_keep_p:]), heads(vc_all[:, seq - c_keep_p:]),
                                 heads(kv[:, :, :W_BRANCH]), heads(kv[:, :, W_BRANCH:]))):
            lst.append(val)
    y_prompt = x.reshape(bsz, seq, D_MODEL)
    prompt_state = [jnp.stack(s) for s in st]

    l_all = past + t_new
    qpos = past + jnp.arange(t_new)
    tabs_s = _rope_tables(jnp.tile(qpos, dbs))
    topk_s = min(TOPK_MAX, l_all // 4)
    kc_s = 512
    lpad_a = -(-l_all // kc_s) * kc_s
    lpad_b = -(-l_all // LANES) * LANES
    wc = c_keep + t_new
    layers_flat = lambda a: a.reshape(depth * dbs, a.shape[2], W_BRANCH)
    cb_k, cb_v = layers_flat(cache_b_k), layers_flat(cache_b_v)
    cc_k, cc_v = layers_flat(cache_c_k), layers_flat(cache_c_v)
    cm_k, cm_v = layers_flat(cache_mem_k), layers_flat(cache_mem_v)
    x = x_sample.reshape(dbs * t_new, D_MODEL)
    ss = [[] for _ in range(8)]
    for l in range(depth):
        p2 = _project(x, g_norm[l].reshape(1, D_MODEL), w_in_p[l], tabs_s, bf_rows[l], dbs * t_new)
        p3 = p2.reshape(dbs, t_new, D_PAD)
        ka, va, kia = misc(p3, M_AK, 64), misc(p3, M_AV, 64), misc(p3, M_AKI, 64)
        lf = misc(p3, M_BF, N_HEADS)
        cat = lambda a, b_: jnp.concatenate([a, b_.astype(a.dtype)], axis=1)
        o_a = _dsa(p3, _pad_rows(cat(cache_a_k[l], ka), lpad_a), _pad_rows(cat(cache_a_v[l], va), lpad_a),
                   _pad_rows(cat(cache_a_kidx[l], kia), lpad_a),
                   tq=t_new, kc=kc_s, l_true=l_all, pos0=past, topk=topk_s)
        kb, vb = slot_of(p3, S_BK), slot_of(p3, S_BV)
        lf_all = jnp.concatenate([cache_b_logf[l].astype(F32), lf], axis=1)
        lf_rows = jnp.pad(jnp.swapaxes(lf_all, 1, 2).reshape(dbs * N_HEADS, l_all), ((0, 0), (0, lpad_b - l_all)))
        f_row = _cumsum_rows(lf_rows).reshape(dbs, N_HEADS, lpad_b)
        f_new = f_row[:, :, past:l_all]
        o_b = _fox_cached(p3, cb_k, cb_v, l, jnp.swapaxes(f_new, 1, 2), f_row[:, :, :past], f_new, tk=past)
        kc_new, vc_new = slot_of(p3, S_CK), slot_of(p3, S_CV)
        bias = _band_bias(c_bias[l], past, past - c_keep, t_new, wc, wc)
        o_c = _win_attention(p3, S_CQ, [(cc_k, 0, cc_v, 0, c_keep, lambda i: 0, l * dbs),
                                        (p3, S_CK, p3, S_CV, t_new, lambda i: 0, 0)],
                             bias, tq=t_new, j_shift=None)
        o_m = _win_attention(p3, S_MQ, [(cm_k, 0, cm_v, 0, n_mem, lambda i: 0, l * dbs)],
                             None, tq=t_new, j_shift=None)
        outs = [o.reshape(dbs * t_new, W_BRANCH) for o in (o_a, o_b, o_c, o_m)]
        x = _merge(x, outs, p2, w_out_b[l], gf, dbs * t_new, l == depth - 1)
        heads = lambda a: a.reshape(a.shape[0], a.shape[1], N_HEADS, HEAD_DIM)
        for lst, val in zip(ss, (ka, va, kia, heads(kb), heads(vb), lf, heads(kc_new), heads(vc_new))):
            lst.append(val)
    y_sample = x.reshape(dbs, t_new, D_MODEL)
    sample_state = [jnp.stack(s) for s in ss]

    return (y_prompt, y_sample, *prompt_state, *sample_state)
```
